```python
import jax, jax.numpy as jnp
from jax import lax
import numpy as np

D_MODEL = 1024
BATCH = 8
SEQ = 2048
DEPTH = 4
DEC_BATCH = 128
DEC_SEQ = 4
PAST_LEN = 16384
PAGE_SIZE = 128

D_MIX = D_MODEL
D_POOL = D_MIX // 4
POOL_WINDOWS = (2, 4, 8, 16)
POOL_GROUPS = len(POOL_WINDOWS)
POOL_GC = D_POOL // POOL_GROUPS
POOL_PAST = max(POOL_WINDOWS) - 1
D_SCONV = (D_MIX - D_POOL) // 2
D_CCONV = D_MIX - D_POOL - D_SCONV
SCONV_K = 3
CCONV_K = 31
FFN_K = 3
D_FF = ((8 * D_MODEL // 3 + 127) // 128) * 128
D_IN = D_POOL + 3 * D_SCONV + 2 * D_CCONV
EPS = 1e-6

kernel_name = "hybrid_pool_shortconv_conformer_decoder_step"


def rms_norm(x, g):
    xf = x.astype(jnp.float32)
    y = xf * lax.rsqrt(jnp.mean(xf * xf, axis=-1, keepdims=True) + EPS)
    return (y * g.astype(jnp.float32)).astype(x.dtype)


def layer_norm(x, g, b):
    xf = x.astype(jnp.float32)
    mu = jnp.mean(xf, axis=-1, keepdims=True)
    var = jnp.mean(jnp.square(xf - mu), axis=-1, keepdims=True)
    y = (xf - mu) * lax.rsqrt(var + EPS) * g.astype(jnp.float32) + b.astype(jnp.float32)
    return y.astype(x.dtype)


def causal_dwconv(z, past, w):
    ext = jnp.concatenate([past.astype(z.dtype), z], axis=1)
    y = lax.conv_general_dilated(ext, w.astype(z.dtype)[:, None, :], (1,), 'VALID',
                                 dimension_numbers=('NWC', 'WIO', 'NWC'),
                                 feature_group_count=z.shape[-1])
    return y, ext[:, -(w.shape[0] - 1):]


def pool_mixer(u, past, pos0, pool_w, pool_scale):
    bsz, L, _ = u.shape
    ext = jnp.concatenate([past.astype(u.dtype), u], axis=1).astype(jnp.float32)
    csum = jnp.concatenate([jnp.zeros_like(ext[:, :1]), jnp.cumsum(ext, axis=1)], axis=1)
    P = POOL_PAST
    pos = pos0 + jnp.arange(L)
    uf = u.astype(jnp.float32)
    outs = []
    for g, w in enumerate(POOL_WINDOWS):
        sl = slice(g * POOL_GC, (g + 1) * POOL_GC)
        win = csum[:, P + 1:P + 1 + L, sl] - csum[:, P + 1 - w:P + 1 - w + L, sl]
        cnt = jnp.minimum(pos + 1, w).astype(jnp.float32)[None, :, None]
        outs.append(win / cnt - uf[..., sl])
    pooled = jnp.stack(outs, axis=2)
    mixed = jnp.einsum('blgc,gcd->blgd', pooled, pool_w.astype(jnp.float32)).reshape(bsz, L, D_POOL)
    out = mixed * pool_scale.astype(jnp.float32)
    return out.astype(u.dtype), ext[:, -P:].astype(u.dtype)


def layer(x, c, past_pool, past_sconv, past_cconv, past_ffn, pos0,
          w_ada, b_ada, g_mix_pre, g_mix_post, g_ffn_pre, g_ffn_post,
          w_in, pool_w, pool_scale, sconv_w, cconv_w, cconv_b, cln_g, cln_b,
          w_out, w_up, ffn_conv_w, w_down):
    mod = (jax.nn.silu(c) @ w_ada + b_ada)[:, None, :]
    sh1, sc1, gt1, sh2, sc2, gt2 = jnp.split(mod, 6, axis=-1)

    h = rms_norm(x, g_mix_pre) * (1 + sc1) + sh1
    proj = h @ w_in
    o = np.cumsum([D_POOL, D_SCONV, D_SCONV, D_SCONV, D_CCONV])
    u_pool, h_b, bg, cg, a_c, b_c = jnp.split(proj, list(o), axis=-1)
    y_a, new_pool = pool_mixer(u_pool, past_pool, pos0, pool_w, pool_scale)
    z_b = cg * h_b
    conv_b, new_sconv = causal_dwconv(z_b, past_sconv, sconv_w)
    y_b = bg * conv_b
    glu = a_c * jax.nn.sigmoid(b_c)
    conv_c, new_cconv = causal_dwconv(glu, past_cconv, cconv_w)
    y_c = jax.nn.silu(layer_norm(conv_c + cconv_b, cln_g, cln_b))
    mix = jnp.concatenate([y_a, y_b, y_c], axis=-1) @ w_out
    x = x + gt1 * rms_norm(mix, g_mix_post)

    h = rms_norm(x, g_ffn_pre) * (1 + sc2) + sh2
    up = h @ w_up
    up_c, new_ffn = causal_dwconv(up, past_ffn, ffn_conv_w)
    gate, val = jnp.split(up_c, 2, axis=-1)
    ff = (jax.nn.silu(gate) * val) @ w_down
    x = x + gt2 * rms_norm(ff, g_ffn_post)
    return x, new_pool, new_sconv, new_cconv, new_ffn


def setup_inputs(seed: int = 0) -> dict:
    key = jax.random.key(seed)
    ks = iter(jax.random.split(key, 32))

    def nrm(shape, s):
        return jax.random.normal(next(ks), shape, jnp.float32) * s

    return {
        "x_prompt": nrm((BATCH, SEQ, D_MODEL), 1.0),
        "x_sample": nrm((DEC_BATCH, DEC_SEQ, D_MODEL), 1.0),
        "c_prompt": nrm((BATCH, D_MODEL), 1.0),
        "c_sample": nrm((DEC_BATCH, D_MODEL), 1.0),
        "state_pool": nrm((DEPTH, DEC_BATCH, POOL_PAST, D_POOL), 1.0),
        "state_sconv": nrm((DEPTH, DEC_BATCH, SCONV_K - 1, D_SCONV), 1.0),
        "state_cconv": nrm((DEPTH, DEC_BATCH, CCONV_K - 1, D_CCONV), 0.5),
        "state_ffn": nrm((DEPTH, DEC_BATCH, FFN_K - 1, 2 * D_FF), 1.0),
        "w_ada": nrm((DEPTH, D_MODEL, 6 * D_MODEL), 0.5 * D_MODEL ** -0.5),
        "b_ada": nrm((DEPTH, 6 * D_MODEL), 0.01),
        "g_mix_pre": 1.0 + nrm((DEPTH, D_MODEL), 0.05),
        "g_mix_post": 1.0 + nrm((DEPTH, D_MODEL), 0.05),
        "g_ffn_pre": 1.0 + nrm((DEPTH, D_MODEL), 0.05),
        "g_ffn_post": 1.0 + nrm((DEPTH, D_MODEL), 0.05),
        "w_in": nrm((DEPTH, D_MODEL, D_IN), D_MODEL ** -0.5),
        "pool_w": nrm((DEPTH, POOL_GROUPS, POOL_GC, POOL_GC), POOL_GC ** -0.5),
        "pool_scale": 1.0 + nrm((DEPTH, D_POOL), 0.1),
        "sconv_w": nrm((DEPTH, SCONV_K, D_SCONV), SCONV_K ** -0.5),
        "cconv_w": nrm((DEPTH, CCONV_K, D_CCONV), CCONV_K ** -0.5),
        "cconv_b": nrm((DEPTH, D_CCONV), 0.01),
        "cln_g": 1.0 + nrm((DEPTH, D_CCONV), 0.05),
        "cln_b": nrm((DEPTH, D_CCONV), 0.01),
        "w_out": nrm((DEPTH, D_MIX, D_MODEL), D_MIX ** -0.5),
        "w_up": nrm((DEPTH, D_MODEL, 2 * D_FF), D_MODEL ** -0.5),
        "ffn_conv_w": nrm((DEPTH, FFN_K, 2 * D_FF), FFN_K ** -0.5),
        "w_down": nrm((DEPTH, D_FF, D_MODEL), D_FF ** -0.5),
    }


def reference(x_prompt, x_sample, c_prompt, c_sample, state_pool, state_sconv, state_cconv, state_ffn,
              w_ada, b_ada, g_mix_pre, g_mix_post, g_ffn_pre, g_ffn_post,
              w_in, pool_w, pool_scale, sconv_w, cconv_w, cconv_b, cln_g, cln_b,
              w_out, w_up, ffn_conv_w, w_down):
    xp, xs = x_prompt, x_sample
    dt = x_prompt.dtype
    zp_pool = jnp.zeros((BATCH, POOL_PAST, D_POOL), dt)
    zp_sconv = jnp.zeros((BATCH, SCONV_K - 1, D_SCONV), dt)
    zp_cconv = jnp.zeros((BATCH, CCONV_K - 1, D_CCONV), dt)
    zp_ffn = jnp.zeros((BATCH, FFN_K - 1, 2 * D_FF), dt)
    np_pool, np_sconv, np_cconv, np_ffn = [], [], [], []
    ns_pool, ns_sconv, ns_cconv, ns_ffn = [], [], [], []
    for l in range(DEPTH):
        params = (w_ada[l], b_ada[l], g_mix_pre[l], g_mix_post[l], g_ffn_pre[l], g_ffn_post[l],
                  w_in[l], pool_w[l], pool_scale[l], sconv_w[l], cconv_w[l], cconv_b[l],
                  cln_g[l], cln_b[l], w_out[l], w_up[l], ffn_conv_w[l], w_down[l])
        xp, a, b, c, d = layer(xp, c_prompt, zp_pool, zp_sconv, zp_cconv, zp_ffn, 0, *params)
        np_pool.append(a); np_sconv.append(b); np_cconv.append(c); np_ffn.append(d)
        xs, a, b, c, d = layer(xs, c_sample, state_pool[l], state_sconv[l], state_cconv[l],
                               state_ffn[l], PAST_LEN, *params)
        ns_pool.append(a); ns_sconv.append(b); ns_cconv.append(c); ns_ffn.append(d)
    return (xp, xs,
            jnp.stack(np_pool), jnp.stack(np_sconv), jnp.stack(np_cconv), jnp.stack(np_ffn),
            jnp.stack(ns_pool), jnp.stack(ns_sconv), jnp.stack(ns_cconv), jnp.stack(ns_ffn))
```

```python
import functools

import jax
import jax.numpy as jnp
from jax import lax
from jax.experimental import pallas as pl
from jax.experimental.pallas import tpu as pltpu

D_MODEL = 1024
BATCH = 8
SEQ = 2048
DEPTH = 4
DEC_BATCH = 128
DEC_SEQ = 4
PAST_LEN = 16384

D_POOL = 256
POOL_WINDOWS = (2, 4, 8, 16)
POOL_GC = 64
POOL_PAST = 15
D_SCONV = 384
D_CCONV = 384
SCONV_K = 3
CCONV_K = 31
FFN_K = 3
D_FF = 2816
D_IN = D_POOL + 3 * D_SCONV + 2 * D_CCONV
EPS = 1e-6

O_HB = D_POOL
O_BG = O_HB + D_SCONV
O_CG = O_BG + D_SCONV
O_AC = O_CG + D_SCONV
O_BC = O_AC + D_CCONV

LANES = 128
SUBLANES = 8
FF_CHUNK = 256
N_FF_CHUNKS = D_FF // FF_CHUNK
TM = 512
N_SEQ_TILES = SEQ // TM
H_POOL = 16
H_SCONV = 8
H_CCONV = 32
H_FFN = 8
DEC_ROWS = DEC_BATCH * DEC_SEQ
N_ADA = BATCH + DEC_BATCH
ADA_TN = 1024
VMEM_LIMIT_BYTES = 56 * 1024 * 1024

F32 = jnp.float32
BF16 = jnp.bfloat16


def _rms(x, g):
    return x * lax.rsqrt(jnp.mean(x * x, axis=-1, keepdims=True) + EPS) * g


def _layer_norm(x, g, b):
    mu = jnp.mean(x, axis=-1, keepdims=True)
    xc = x - mu
    var = jnp.mean(xc * xc, axis=-1, keepdims=True)
    return xc * lax.rsqrt(var + EPS) * g + b


def _silu(x):
    return x * jax.nn.sigmoid(x)


def _dot(a, b):
    return jnp.dot(a, b, preferred_element_type=F32)


def _ada_kernel(c_ref, w_ref, b_ref, o_ref):
    s = _silu(c_ref[...]).astype(BF16)
    o_ref[0] = _dot(s, w_ref[0].astype(BF16)) + b_ref[0]


def _ada_call(c_all, w_ada, b_ada):
    return pl.pallas_call(
        _ada_kernel,
        grid=(DEPTH, 6 * D_MODEL // ADA_TN),
        in_specs=[
            pl.BlockSpec((N_ADA, D_MODEL), lambda l, n: (0, 0)),
            pl.BlockSpec((1, D_MODEL, ADA_TN), lambda l, n: (l, 0, n)),
            pl.BlockSpec((1, 1, ADA_TN), lambda l, n: (l, 0, n)),
        ],
        out_specs=pl.BlockSpec((1, N_ADA, ADA_TN), lambda l, n: (l, 0, n)),
        out_shape=jax.ShapeDtypeStruct((DEPTH, N_ADA, 6 * D_MODEL), F32),
        compiler_params=pltpu.CompilerParams(
            dimension_semantics=("arbitrary", "arbitrary"),
            vmem_limit_bytes=VMEM_LIMIT_BYTES),
        name="ada",
    )(c_all, w_ada, b_ada.reshape(DEPTH, 1, 6 * D_MODEL))


def _mixer_heads(proj):
    u = proj[:, 0:O_HB]
    h_b = proj[:, O_HB:O_BG]
    bg = proj[:, O_BG:O_CG]
    cg = proj[:, O_CG:O_AC]
    a_c = proj[:, O_AC:O_BC]
    b_c = proj[:, O_BC:D_IN]
    return u, cg * h_b, bg, a_c * jax.nn.sigmoid(b_c)


def _mixer_tail(x, gt1, gpost, win, cnt, u, conv_b, bg, conv_c,
                wpool_ref, pscale_ref, cb_ref, lg_ref, lb_ref, wout_ref):
    pooled = win / cnt - u
    y_a = _dot(pooled.astype(BF16), wpool_ref[...]) * pscale_ref[...]
    y_b = bg * conv_b
    y_c = _silu(_layer_norm(conv_c + cb_ref[...], lg_ref[...], lb_ref[...]))
    y = jnp.concatenate([y_a, y_b, y_c], axis=-1).astype(BF16)
    mix = _dot(y, wout_ref[...])
    return x + gt1 * _rms(mix, gpost)


def _mix_prompt_kernel(x_ref, mod_ref, gpre_ref, gpost_ref, win_ref, wpool_ref, pscale_ref,
                       pw_ref, wl_ref, sw_ref, cw_ref, cb_ref, lg_ref, lb_ref, wout_ref,
                       xo_ref, tpool_ref, tz_ref, tglu_ref,
                       epool, ez, eglu):
    t = pl.program_id(1)

    @pl.when(t == 0)
    def _():
        epool[0:H_POOL, :] = jnp.zeros((H_POOL, D_POOL), F32)
        ez[0:H_SCONV, :] = jnp.zeros((H_SCONV, D_SCONV), F32)
        eglu[0:H_CCONV, :] = jnp.zeros((H_CCONV, D_CCONV), F32)

    x = x_ref[0]
    sh1 = mod_ref[0, :, 0:D_MODEL]
    sc1 = mod_ref[0, :, D_MODEL:2 * D_MODEL]
    gt1 = mod_ref[0, :, 2 * D_MODEL:3 * D_MODEL]
    h = _rms(x, gpre_ref[...]) * (1.0 + sc1) + sh1
    proj = _dot(h.astype(BF16), win_ref[...])
    u, z_b, bg, glu = _mixer_heads(proj)
    epool[H_POOL:H_POOL + TM, :] = u
    ez[H_SCONV:H_SCONV + TM, :] = z_b
    eglu[H_CCONV:H_CCONV + TM, :] = glu

    win0 = pw_ref[0:1, 0:LANES] * u[:, 0:LANES]
    for i in range(1, POOL_WINDOWS[1]):
        win0 = win0 + pw_ref[i:i + 1, 0:LANES] * epool[pl.ds(H_POOL - i, TM), 0:LANES]
    win1 = pw_ref[0:1, LANES:D_POOL] * u[:, LANES:D_POOL]
    for i in range(1, POOL_WINDOWS[3]):
        win1 = win1 + pw_ref[i:i + 1, LANES:D_POOL] * epool[pl.ds(H_POOL - i, TM), LANES:D_POOL]
    win = jnp.concatenate([win0, win1], axis=-1)
    pos1 = lax.broadcasted_iota(jnp.int32, (TM, D_POOL), 0) + (t * TM + 1)
    cnt = jnp.minimum(pos1.astype(F32), wl_ref[...])

    conv_b = sw_ref[SCONV_K - 1:SCONV_K, :] * z_b
    for k in range(SCONV_K - 1):
        conv_b = conv_b + sw_ref[k:k + 1, :] * ez[pl.ds(H_SCONV - (SCONV_K - 1) + k, TM), :]

    conv_c = cw_ref[CCONV_K - 1:CCONV_K, :] * glu
    for k in range(CCONV_K - 1):
        conv_c = conv_c + cw_ref[k:k + 1, :] * eglu[pl.ds(H_CCONV - (CCONV_K - 1) + k, TM), :]

    xo_ref[0] = _mixer_tail(x, gt1, gpost_ref[...], win, cnt, u, conv_b, bg, conv_c,
                            wpool_ref, pscale_ref, cb_ref, lg_ref, lb_ref, wout_ref)

    epool[0:H_POOL, :] = epool[TM:TM + H_POOL, :]
    ez[0:H_SCONV, :] = ez[TM:TM + H_SCONV, :]
    eglu[0:H_CCONV, :] = eglu[TM:TM + H_CCONV, :]

    @pl.when(t == N_SEQ_TILES - 1)
    def _():
        tpool_ref[0] = epool[TM:TM + H_POOL, :]
        tz_ref[0] = ez[TM:TM + H_SCONV, :]
        tglu_ref[0] = eglu[TM:TM + H_CCONV, :]


def _const_spec(shape):
    return pl.BlockSpec(shape, lambda b, t: (0,) * len(shape))


def _mix_prompt_call(x, mod, p):
    tile = pl.BlockSpec((1, TM, D_MODEL), lambda b, t: (b, t, 0))
    tail = lambda rows, ch: pl.BlockSpec((1, rows, ch), lambda b, t: (b, 0, 0))
    return pl.pallas_call(
        _mix_prompt_kernel,
        grid=(BATCH, N_SEQ_TILES),
        in_specs=[
            tile,
            pl.BlockSpec((1, 1, 6 * D_MODEL), lambda b, t: (b, 0, 0)),
            _const_spec((1, D_MODEL)), _const_spec((1, D_MODEL)),
            _const_spec((D_MODEL, D_IN)),
            _const_spec((D_POOL, D_POOL)), _const_spec((1, D_POOL)),
            _const_spec((H_POOL, D_POOL)), _const_spec((1, D_POOL)),
            _const_spec((SCONV_K, D_SCONV)),
            _const_spec((CCONV_K, D_CCONV)), _const_spec((1, D_CCONV)),
            _const_spec((1, D_CCONV)), _const_spec((1, D_CCONV)),
            _const_spec((D_MODEL, D_MODEL)),
        ],
        out_specs=[tile, tail(H_POOL, D_POOL), tail(H_SCONV, D_SCONV), tail(H_CCONV, D_CCONV)],
        out_shape=[
            jax.ShapeDtypeStruct((BATCH, SEQ, D_MODEL), F32),
            jax.ShapeDtypeStruct((BATCH, H_POOL, D_POOL), F32),
            jax.ShapeDtypeStruct((BATCH, H_SCONV, D_SCONV), F32),
            jax.ShapeDtypeStruct((BATCH, H_CCONV, D_CCONV), F32),
        ],
        scratch_shapes=[
            pltpu.VMEM((H_POOL + TM, D_POOL), F32),
            pltpu.VMEM((H_SCONV + TM, D_SCONV), F32),
            pltpu.VMEM((H_CCONV + TM, D_CCONV), F32),
        ],
        compiler_params=pltpu.CompilerParams(
            dimension_semantics=("arbitrary", "arbitrary"),
            vmem_limit_bytes=VMEM_LIMIT_BYTES),
        name="mix_prompt",
    )(x, mod, p["g_mix_pre"], p["g_mix_post"], p["w_in"], p["wpool"], p["pool_scale"],
      p["pw"], p["wl"], p["sconv_w"], p["cconv_w"], p["cconv_b"], p["cln_g"], p["cln_b"],
      p["w_out"])


def _ffn_prompt_kernel(x_ref, mod_ref, gpre_ref, gpost_ref, wg_ref, wv_ref, fw_ref, wd_ref,
                       xo_ref, tup_ref, eup):
    t = pl.program_id(1)

    @pl.when(t == 0)
    def _():
        eup[0:H_FFN, :] = jnp.zeros((H_FFN, 2 * D_FF), F32)

    x = x_ref[0]
    sh2 = mod_ref[0, :, 3 * D_MODEL:4 * D_MODEL]
    sc2 = mod_ref[0, :, 4 * D_MODEL:5 * D_MODEL]
    gt2 = mod_ref[0, :, 5 * D_MODEL:6 * D_MODEL]
    hb = (_rms(x, gpre_ref[...]) * (1.0 + sc2) + sh2).astype(BF16)

    def conv(cur, lo):
        eup[H_FFN:H_FFN + TM, lo:lo + FF_CHUNK] = cur
        out = fw_ref[FFN_K - 1:FFN_K, lo:lo + FF_CHUNK] * cur
        for k in range(FFN_K - 1):
            out = out + (fw_ref[k:k + 1, lo:lo + FF_CHUNK]
                         * eup[pl.ds(H_FFN - (FFN_K - 1) + k, TM), lo:lo + FF_CHUNK])
        return out

    ff = None
    for j in range(N_FF_CHUNKS):
        gate = conv(_dot(hb, wg_ref[j]), j * FF_CHUNK)
        val = conv(_dot(hb, wv_ref[j]), D_FF + j * FF_CHUNK)
        part = _dot((_silu(gate) * val).astype(BF16), wd_ref[j])
        ff = part if ff is None else ff + part

    xo_ref[0] = x + gt2 * _rms(ff, gpost_ref[...])
    eup[0:H_FFN, :] = eup[TM:TM + H_FFN, :]

    @pl.when(t == N_SEQ_TILES - 1)
    def _():
        tup_ref[0] = eup[TM:TM + H_FFN, :]


def _ffn_prompt_call(x, mod, p):
    tile = pl.BlockSpec((1, TM, D_MODEL), lambda b, t: (b, t, 0))
    once = pl.Buffered(1)
    return pl.pallas_call(
        _ffn_prompt_kernel,
        grid=(BATCH, N_SEQ_TILES),
        in_specs=[
            tile,
            pl.BlockSpec((1, 1, 6 * D_MODEL), lambda b, t: (b, 0, 0)),
            _const_spec((1, D_MODEL)), _const_spec((1, D_MODEL)),
            pl.BlockSpec((N_FF_CHUNKS, D_MODEL, FF_CHUNK), lambda b, t: (0, 0, 0), pipeline_mode=once),
            pl.BlockSpec((N_FF_CHUNKS, D_MODEL, FF_CHUNK), lambda b, t: (0, 0, 0), pipeline_mode=once),
            _const_spec((FFN_K, 2 * D_FF)),
            pl.BlockSpec((N_FF_CHUNKS, FF_CHUNK, D_MODEL), lambda b, t: (0, 0, 0), pipeline_mode=once),
        ],
        out_specs=[tile, pl.BlockSpec((1, H_FFN, 2 * D_FF), lambda b, t: (b, 0, 0))],
        out_shape=[
            jax.ShapeDtypeStruct((BATCH, SEQ, D_MODEL), F32),
            jax.ShapeDtypeStruct((BATCH, H_FFN, 2 * D_FF), F32),
        ],
        scratch_shapes=[pltpu.VMEM((H_FFN + TM, 2 * D_FF), F32)],
        compiler_params=pltpu.CompilerParams(
            dimension_semantics=("arbitrary", "arbitrary"),
            vmem_limit_bytes=VMEM_LIMIT_BYTES),
        name="ffn_prompt",
    )(x, mod, p["g_ffn_pre"], p["g_ffn_post"], p["w_up_g"], p["w_up_v"], p["ffn_conv_w"],
      p["w_down"])


def _slab(a, t):
    return a[t * DEC_BATCH:(t + 1) * DEC_BATCH]


def _mix_dec_kernel(x_ref, mod_ref, gpre_ref, gpost_ref, win_ref, wpool_ref, pscale_ref,
                    pw_ref, wl_ref, sw_ref, cw_ref, cb_ref, lg_ref, lb_ref, wout_ref,
                    spool_ref, sz_ref, sglu_ref,
                    xo_ref, u_ref, z_ref, glu_ref):
    x = x_ref[...]
    rep = lambda a: jnp.concatenate([a] * DEC_SEQ, axis=0)
    sh1 = rep(mod_ref[:, 0:D_MODEL])
    sc1 = rep(mod_ref[:, D_MODEL:2 * D_MODEL])
    gt1 = rep(mod_ref[:, 2 * D_MODEL:3 * D_MODEL])
    h = _rms(x, gpre_ref[...]) * (1.0 + sc1) + sh1
    proj = _dot(h.astype(BF16), win_ref[...])
    u, z_b, bg, glu = _mixer_heads(proj)
    u_ref[...] = u
    z_ref[...] = z_b[(DEC_SEQ - (SCONV_K - 1)) * DEC_BATCH:]
    glu_ref[...] = glu

    def ext(state_ref, cur, n_past, j):
        return state_ref[j] if j < n_past else _slab(cur, j - n_past)

    wins, convs_b, convs_c, cnts = [], [], [], []
    for t in range(DEC_SEQ):
        acc = None
        for i in range(POOL_WINDOWS[3]):
            term = pw_ref[i:i + 1, :] * ext(spool_ref, u, POOL_PAST, POOL_PAST + t - i)
            acc = term if acc is None else acc + term
        wins.append(acc)
        cnts.append(jnp.broadcast_to(jnp.minimum(F32(PAST_LEN + t + 1), wl_ref[...]),
                                     (DEC_BATCH, D_POOL)))
        acc = None
        for k in range(SCONV_K):
            term = sw_ref[k:k + 1, :] * ext(sz_ref, z_b, SCONV_K - 1, t + k)
            acc = term if acc is None else acc + term
        convs_b.append(acc)
        acc = None
        for k in range(CCONV_K):
            term = cw_ref[k:k + 1, :] * ext(sglu_ref, glu, CCONV_K - 1, t + k)
            acc = term if acc is None else acc + term
        convs_c.append(acc)
    cat = lambda xs: jnp.concatenate(xs, axis=0)
    xo_ref[...] = _mixer_tail(x, gt1, gpost_ref[...], cat(wins), cat(cnts), u, cat(convs_b), bg,
                              cat(convs_c), wpool_ref, pscale_ref, cb_ref, lg_ref, lb_ref,
                              wout_ref)


def _full_spec(shape):
    return pl.BlockSpec(shape, lambda i: (0,) * len(shape))


def _mix_dec_call(x, mod, spool, sz, sglu, p):
    return pl.pallas_call(
        _mix_dec_kernel,
        grid=(1,),
        in_specs=[
            _full_spec((DEC_ROWS, D_MODEL)), _full_spec((DEC_BATCH, 6 * D_MODEL)),
            _full_spec((1, D_MODEL)), _full_spec((1, D_MODEL)),
            _full_spec((D_MODEL, D_IN)),
            _full_spec((D_POOL, D_POOL)), _full_spec((1, D_POOL)),
            _full_spec((H_POOL, D_POOL)), _full_spec((1, D_POOL)),
            _full_spec((SCONV_K, D_SCONV)),
            _full_spec((CCONV_K, D_CCONV)), _full_spec((1, D_CCONV)),
            _full_spec((1, D_CCONV)), _full_spec((1, D_CCONV)),
            _full_spec((D_MODEL, D_MODEL)),
            _full_spec((POOL_PAST, DEC_BATCH, D_POOL)),
            _full_spec((SCONV_K - 1, DEC_BATCH, D_SCONV)),
            _full_spec((CCONV_K - 1, DEC_BATCH, D_CCONV)),
        ],
        out_specs=[
            _full_spec((DEC_ROWS, D_MODEL)), _full_spec((DEC_ROWS, D_POOL)),
            _full_spec(((SCONV_K - 1) * DEC_BATCH, D_SCONV)), _full_spec((DEC_ROWS, D_CCONV)),
        ],
        out_shape=[
            jax.ShapeDtypeStruct((DEC_ROWS, D_MODEL), F32),
            jax.ShapeDtypeStruct((DEC_ROWS, D_POOL), F32),
            jax.ShapeDtypeStruct(((SCONV_K - 1) * DEC_BATCH, D_SCONV), F32),
            jax.ShapeDtypeStruct((DEC_ROWS, D_CCONV), F32),
        ],
        compiler_params=pltpu.CompilerParams(
            dimension_semantics=("arbitrary",), vmem_limit_bytes=VMEM_LIMIT_BYTES),
        name="mix_dec",
    )(x, mod, p["g_mix_pre"], p["g_mix_post"], p["w_in"], p["wpool"], p["pool_scale"],
      p["pw"], p["wl"], p["sconv_w"], p["cconv_w"], p["cconv_b"], p["cln_g"], p["cln_b"],
      p["w_out"], spool, sz, sglu)


def _ffn_dec_kernel(x_ref, mod_ref, gpre_ref, gpost_ref, wg_ref, wv_ref, fw_ref, wd_ref, sup_ref,
                    xo_ref, up_ref):
    x = x_ref[...]
    rep = lambda a: jnp.concatenate([a] * DEC_SEQ, axis=0)
    sh2 = rep(mod_ref[:, 3 * D_MODEL:4 * D_MODEL])
    sc2 = rep(mod_ref[:, 4 * D_MODEL:5 * D_MODEL])
    gt2 = rep(mod_ref[:, 5 * D_MODEL:6 * D_MODEL])
    hb = (_rms(x, gpre_ref[...]) * (1.0 + sc2) + sh2).astype(BF16)
    n_past = FFN_K - 1

    def conv(cur, lo):
        up_ref[:, lo:lo + FF_CHUNK] = cur[(DEC_SEQ - n_past) * DEC_BATCH:]
        outs = []
        for t in range(DEC_SEQ):
            acc = None
            for k in range(FFN_K):
                j = t + k
                row = sup_ref[j, :, lo:lo + FF_CHUNK] if j < n_past else _slab(cur, j - n_past)
                term = fw_ref[k:k + 1, lo:lo + FF_CHUNK] * row
                acc = term if acc is None else acc + term
            outs.append(acc)
        return jnp.concatenate(outs, axis=0)

    ff = None
    for j in range(N_FF_CHUNKS):
        gate = conv(_dot(hb, wg_ref[j]), j * FF_CHUNK)
        val = conv(_dot(hb, wv_ref[j]), D_FF + j * FF_CHUNK)
        part = _dot((_silu(gate) * val).astype(BF16), wd_ref[j])
        ff = part if ff is None else ff + part
    xo_ref[...] = x + gt2 * _rms(ff, gpost_ref[...])


def _ffn_dec_call(x, mod, sup, p):
    n_past = FFN_K - 1
    return pl.pallas_call(
        _ffn_dec_kernel,
        grid=(1,),
        in_specs=[
            _full_spec((DEC_ROWS, D_MODEL)), _full_spec((DEC_BATCH, 6 * D_MODEL)),
            _full_spec((1, D_MODEL)), _full_spec((1, D_MODEL)),
            _full_spec((N_FF_CHUNKS, D_MODEL, FF_CHUNK)),
            _full_spec((N_FF_CHUNKS, D_MODEL, FF_CHUNK)),
            _full_spec((FFN_K, 2 * D_FF)),
            _full_spec((N_FF_CHUNKS, FF_CHUNK, D_MODEL)),
            _full_spec((n_past, DEC_BATCH, 2 * D_FF)),
        ],
        out_specs=[_full_spec((DEC_ROWS, D_MODEL)), _full_spec((n_past * DEC_BATCH, 2 * D_FF))],
        out_shape=[
            jax.ShapeDtypeStruct((DEC_ROWS, D_MODEL), F32),
            jax.ShapeDtypeStruct((n_past * DEC_BATCH, 2 * D_FF), F32),
        ],
        compiler_params=pltpu.CompilerParams(
            dimension_semantics=("arbitrary",), vmem_limit_bytes=VMEM_LIMIT_BYTES),
        name="ffn_dec",
    )(x, mod, p["g_ffn_pre"], p["g_ffn_post"], p["w_up_g"], p["w_up_v"], p["ffn_conv_w"],
      p["w_down"], sup)


def _block_diag(pool_w):
    out = jnp.zeros((D_POOL, D_POOL), pool_w.dtype)
    for g in range(len(POOL_WINDOWS)):
        out = out.at[g * POOL_GC:(g + 1) * POOL_GC, g * POOL_GC:(g + 1) * POOL_GC].set(pool_w[g])
    return out


def _pool_constants():
    wl = jnp.repeat(jnp.asarray(POOL_WINDOWS, F32), POOL_GC)[None, :]
    pw = (jnp.arange(H_POOL, dtype=F32)[:, None] < wl).astype(F32)
    return pw, wl


def _to_pos_major(a):
    return jnp.swapaxes(a, 0, 1)


def kernel(x_prompt, x_sample, c_prompt, c_sample, state_pool, state_sconv, state_cconv, state_ffn,
           w_ada, b_ada, g_mix_pre, g_mix_post, g_ffn_pre, g_ffn_post, w_in, pool_w, pool_scale,
           sconv_w, cconv_w, cconv_b, cln_g, cln_b, w_out, w_up, ffn_conv_w, w_down):
    mod = _ada_call(jnp.concatenate([c_prompt, c_sample], axis=0), w_ada, b_ada)
    pw, wl = _pool_constants()
    row = lambda a: a[None, :]

    xp = x_prompt
    xs = _to_pos_major(x_sample).reshape(DEC_ROWS, D_MODEL)
    outs = [[] for _ in range(8)]
    for l in range(DEPTH):
        p = dict(
            g_mix_pre=row(g_mix_pre[l]), g_mix_post=row(g_mix_post[l]),
            g_ffn_pre=row(g_ffn_pre[l]), g_ffn_post=row(g_ffn_post[l]),
            w_in=w_in[l].astype(BF16), wpool=_block_diag(pool_w[l]).astype(BF16),
            pool_scale=row(pool_scale[l]), pw=pw, wl=wl,
            sconv_w=sconv_w[l], cconv_w=cconv_w[l], cconv_b=row(cconv_b[l]),
            cln_g=row(cln_g[l]), cln_b=row(cln_b[l]),
            w_out=w_out[l].astype(BF16),
            w_up_g=w_up[l][:, :D_FF].astype(BF16).reshape(D_MODEL, N_FF_CHUNKS, FF_CHUNK).swapaxes(0, 1),
            w_up_v=w_up[l][:, D_FF:].astype(BF16).reshape(D_MODEL, N_FF_CHUNKS, FF_CHUNK).swapaxes(0, 1),
            ffn_conv_w=ffn_conv_w[l],
            w_down=w_down[l].astype(BF16).reshape(N_FF_CHUNKS, FF_CHUNK, D_MODEL),
        )
        mod_p = mod[l, :BATCH].reshape(BATCH, 1, 6 * D_MODEL)
        mod_s = mod[l, BATCH:]

        xp, tpool, tz, tglu = _mix_prompt_call(xp, mod_p, p)
        xp, tup = _ffn_prompt_call(xp, mod_p, p)
        outs[0].append(tpool[:, H_POOL - POOL_PAST:])
        outs[1].append(tz[:, H_SCONV - (SCONV_K - 1):])
        outs[2].append(tglu[:, H_CCONV - (CCONV_K - 1):])
        outs[3].append(tup[:, H_FFN - (FFN_K - 1):])

        xs, u, z, glu = _mix_dec_call(
            xs, mod_s, _to_pos_major(state_pool[l]), _to_pos_major(state_sconv[l]),
            _to_pos_major(state_cconv[l]), p)
        xs, up = _ffn_dec_call(xs, mod_s, _to_pos_major(state_ffn[l]), p)
        seq_major = lambda a, n: _to_pos_major(a.reshape(n, DEC_BATCH, a.shape[-1]))
        outs[4].append(jnp.concatenate([state_pool[l][:, DEC_SEQ:], seq_major(u, DEC_SEQ)], axis=1))
        outs[5].append(seq_major(z, SCONV_K - 1))
        outs[6].append(jnp.concatenate([state_cconv[l][:, DEC_SEQ:], seq_major(glu, DEC_SEQ)], axis=1))
        outs[7].append(seq_major(up, FFN_K - 1))

    ys = _to_pos_major(xs.reshape(DEC_SEQ, DEC_BATCH, D_MODEL))
    return (xp, ys) + tuple(jnp.stack(o) for o in outs)
```

```python
import functools

import jax
import jax.numpy as jnp
from jax import lax
from jax.experimental import pallas as pl
from jax.experimental.pallas import tpu as pltpu

D_MODEL = 1024
BATCH = 8
SEQ = 2048
DEPTH = 4
DEC_BATCH = 128
DEC_SEQ = 4
PAST_LEN = 16384

D_POOL = 256
POOL_WINDOWS = (2, 4, 8, 16)
POOL_GC = 64
POOL_PAST = 15
D_SCONV = 384
D_CCONV = 384
SCONV_K = 3
CCONV_K = 31
FFN_K = 3
D_FF = 2816
D_IN = D_POOL + 3 * D_SCONV + 2 * D_CCONV
EPS = 1e-6

O_HB = D_POOL
O_BG = O_HB + D_SCONV
O_CG = O_BG + D_SCONV
O_AC = O_CG + D_SCONV
O_BC = O_AC + D_CCONV

LANES = 128
SUBLANES = 8
FF_CHUNK = 256
N_FF_CHUNKS = D_FF // FF_CHUNK
TM = 512
N_SEQ_TILES = SEQ // TM
SEG = TM // SUBLANES
DEC_ROWS = DEC_BATCH * DEC_SEQ
N_ADA = BATCH + DEC_BATCH
ADA_TN = 1024
VMEM_LIMIT_BYTES = 56 * 1024 * 1024

F32 = jnp.float32
BF16 = jnp.bfloat16


def _rms(x, g):
    return x * lax.rsqrt(jnp.mean(x * x, axis=-1, keepdims=True) + EPS) * g


def _layer_norm(x, g, b):
    mu = jnp.mean(x, axis=-1, keepdims=True)
    xc = x - mu
    var = jnp.mean(xc * xc, axis=-1, keepdims=True)
    return xc * lax.rsqrt(var + EPS) * g + b


def _silu(x):
    return x * jax.nn.sigmoid(x)


def _dot(a, b):
    return jnp.dot(a, b, preferred_element_type=F32)


def _layer_spec(shape, l):
    return pl.BlockSpec((None,) + shape, lambda *_: (l,) + (0,) * len(shape))


def _ada_kernel(c_ref, w_ref, b_ref, o_ref):
    s = _silu(c_ref[...]).astype(BF16)
    o_ref[...] = _dot(s, w_ref[...].astype(BF16)) + b_ref[...]


def _ada_call(c_all, w_ada, b_ada):
    return pl.pallas_call(
        _ada_kernel,
        grid=(DEPTH, 6 * D_MODEL // ADA_TN),
        in_specs=[
            pl.BlockSpec((N_ADA, D_MODEL), lambda l, n: (0, 0)),
            pl.BlockSpec((None, D_MODEL, ADA_TN), lambda l, n: (l, 0, n)),
            pl.BlockSpec((None, 1, ADA_TN), lambda l, n: (l, 0, n)),
        ],
        out_specs=pl.BlockSpec((None, N_ADA, ADA_TN), lambda l, n: (l, 0, n)),
        out_shape=jax.ShapeDtypeStruct((DEPTH, N_ADA, 6 * D_MODEL), F32),
        compiler_params=pltpu.CompilerParams(
            dimension_semantics=("arbitrary", "arbitrary"),
            vmem_limit_bytes=VMEM_LIMIT_BYTES),
        name="ada",
    )(c_all, w_ada, b_ada.reshape(DEPTH, 1, 6 * D_MODEL))


def _mixer_heads(proj):
    u = proj[:, 0:O_HB]
    h_b = proj[:, O_HB:O_BG]
    bg = proj[:, O_BG:O_CG]
    cg = proj[:, O_CG:O_AC]
    a_c = proj[:, O_AC:O_BC]
    b_c = proj[:, O_BC:D_IN]
    return u, cg * h_b, bg, a_c * jax.nn.sigmoid(b_c)


def _mixer_tail(x, gt1, gpost, win, cnt, u, conv_b, bg, conv_c,
                wpool_ref, pscale_ref, cb_ref, lg_ref, lb_ref, wout_ref):
    pooled = win / cnt - u
    y_a = _dot(pooled.astype(BF16), wpool_ref[...]) * pscale_ref[...]
    y_b = bg * conv_b
    y_c = _silu(_layer_norm(conv_c + cb_ref[...], lg_ref[...], lb_ref[...]))
    y = jnp.concatenate([y_a, y_b, y_c], axis=-1).astype(BF16)
    mix = _dot(y, wout_ref[...])
    return x + gt1 * _rms(mix, gpost)


def _halo_groups(prev_tail, cur, n_back):
    tail = cur[TM - SUBLANES * n_back:]
    sub = lax.broadcasted_iota(jnp.int32, tail.shape, 0) & (SUBLANES - 1)
    mixed = jnp.where(sub == SUBLANES - 1, prev_tail, tail)
    groups = [pltpu.roll(mixed[SUBLANES * i:SUBLANES * (i + 1)], 1, axis=0) for i in range(n_back)]
    return jnp.concatenate(groups, axis=0)


def _back(ext, n_back, j):
    lo = SUBLANES * (n_back - j)
    return ext[lo:lo + TM]


def _mix_prompt_kernel(x_ref, mod_ref, gpre_ref, gpost_ref, win_ref, wpool_ref, pscale_ref,
                       pw_ref, wl_ref, sw_ref, cw_ref, cb_ref, lg_ref, lb_ref, wout_ref,
                       xo_ref, tpool_ref, tz_ref, tglu_ref,
                       tail_u, tail_z, tail_glu):
    b = pl.program_id(0)
    t = pl.program_id(1)

    @pl.when(t == 0)
    def _():
        tail_u[...] = jnp.zeros(tail_u.shape, F32)
        tail_z[...] = jnp.zeros(tail_z.shape, F32)
        tail_glu[...] = jnp.zeros(tail_glu.shape, F32)

    x = x_ref[0]
    sh1 = mod_ref[pl.ds(b, 1), 0:D_MODEL]
    sc1 = mod_ref[pl.ds(b, 1), D_MODEL:2 * D_MODEL]
    gt1 = mod_ref[pl.ds(b, 1), 2 * D_MODEL:3 * D_MODEL]
    h = _rms(x, gpre_ref[...]) * (1.0 + sc1) + sh1
    proj = _dot(h.astype(BF16), win_ref[...])
    u, z_b, bg, glu = _mixer_heads(proj)

    eu = jnp.concatenate([_halo_groups(tail_u[...], u, POOL_PAST), u], axis=0)
    ez = jnp.concatenate([_halo_groups(tail_z[...], z_b, SCONV_K - 1), z_b], axis=0)
    eg = jnp.concatenate([_halo_groups(tail_glu[...], glu, CCONV_K - 1), glu], axis=0)
    tail_u[...] = u[TM - SUBLANES * POOL_PAST:]
    tail_z[...] = z_b[TM - SUBLANES * (SCONV_K - 1):]
    tail_glu[...] = glu[TM - SUBLANES * (CCONV_K - 1):]

    win0 = pw_ref[0:1, 0:LANES] * u[:, 0:LANES]
    for i in range(1, POOL_WINDOWS[1]):
        win0 = win0 + pw_ref[i:i + 1, 0:LANES] * _back(eu, POOL_PAST, i)[:, 0:LANES]
    win1 = pw_ref[0:1, LANES:D_POOL] * u[:, LANES:D_POOL]
    for i in range(1, POOL_WINDOWS[3]):
        win1 = win1 + pw_ref[i:i + 1, LANES:D_POOL] * _back(eu, POOL_PAST, i)[:, LANES:D_POOL]
    win = jnp.concatenate([win0, win1], axis=-1)
    row = lax.broadcasted_iota(jnp.int32, (TM, D_POOL), 0)
    pos1 = t * TM + 1 + (row & (SUBLANES - 1)) * SEG + (row >> 3)
    cnt = jnp.minimum(pos1.astype(F32), wl_ref[...])

    conv_b = sw_ref[SCONV_K - 1:SCONV_K, :] * z_b
    for k in range(SCONV_K - 1):
        conv_b = conv_b + sw_ref[k:k + 1, :] * _back(ez, SCONV_K - 1, SCONV_K - 1 - k)

    conv_c = cw_ref[CCONV_K - 1:CCONV_K, :] * glu
    for k in range(CCONV_K - 1):
        conv_c = conv_c + cw_ref[k:k + 1, :] * _back(eg, CCONV_K - 1, CCONV_K - 1 - k)

    xo_ref[0] = _mixer_tail(x, gt1, gpost_ref[...], win, cnt, u, conv_b, bg, conv_c,
                            wpool_ref, pscale_ref, cb_ref, lg_ref, lb_ref, wout_ref)

    @pl.when(t == N_SEQ_TILES - 1)
    def _():
        tpool_ref[0] = u[TM - SUBLANES * POOL_PAST:]
        tz_ref[0] = z_b[TM - SUBLANES * (SCONV_K - 1):]
        tglu_ref[0] = glu[TM - SUBLANES * (CCONV_K - 1):]


def _mix_prompt_call(l, x, mod, p):
    tile = pl.BlockSpec((1, TM, D_MODEL), lambda b, t: (b, t, 0))
    tail = lambda n_back, ch: pl.BlockSpec((1, SUBLANES * n_back, ch), lambda b, t: (b, 0, 0))
    tail_shape = lambda n_back, ch: (BATCH, SUBLANES * n_back, ch)
    ls = functools.partial(_layer_spec, l=l)
    return pl.pallas_call(
        _mix_prompt_kernel,
        grid=(BATCH, N_SEQ_TILES),
        in_specs=[
            tile,
            pl.BlockSpec((None, BATCH, 6 * D_MODEL), lambda b, t: (l, DEC_BATCH // BATCH, 0)),
            ls((1, D_MODEL)), ls((1, D_MODEL)),
            ls((D_MODEL, D_IN)),
            ls((D_POOL, D_POOL)), ls((1, D_POOL)),
            pl.BlockSpec((2 * SUBLANES, D_POOL), lambda b, t: (0, 0)),
            pl.BlockSpec((1, D_POOL), lambda b, t: (0, 0)),
            ls((SCONV_K, D_SCONV)),
            ls((CCONV_K, D_CCONV)), ls((1, D_CCONV)), ls((1, D_CCONV)), ls((1, D_CCONV)),
            ls((D_MODEL, D_MODEL)),
        ],
        out_specs=[tile, tail(POOL_PAST, D_POOL), tail(SCONV_K - 1, D_SCONV),
                   tail(CCONV_K - 1, D_CCONV)],
        out_shape=[
            jax.ShapeDtypeStruct((BATCH, SEQ, D_MODEL), F32),
            jax.ShapeDtypeStruct(tail_shape(POOL_PAST, D_POOL), F32),
            jax.ShapeDtypeStruct(tail_shape(SCONV_K - 1, D_SCONV), F32),
            jax.ShapeDtypeStruct(tail_shape(CCONV_K - 1, D_CCONV), F32),
        ],
        scratch_shapes=[
            pltpu.VMEM((SUBLANES * POOL_PAST, D_POOL), F32),
            pltpu.VMEM((SUBLANES * (SCONV_K - 1), D_SCONV), F32),
            pltpu.VMEM((SUBLANES * (CCONV_K - 1), D_CCONV), F32),
        ],
        compiler_params=pltpu.CompilerParams(
            dimension_semantics=("arbitrary", "arbitrary"),
            vmem_limit_bytes=VMEM_LIMIT_BYTES),
        name="mix_prompt",
    )(x, mod, p["g_mix_pre"], p["g_mix_post"], p["w_in"], p["wpool"], p["pool_scale"],
      p["pw"], p["wl"], p["sconv_w"], p["cconv_w"], p["cconv_b"], p["cln_g"], p["cln_b"],
      p["w_out"])


def _ffn_prompt_kernel(x_ref, mod_ref, gpre_ref, gpost_ref, wu_ref, fw_ref, wd_ref,
                       xo_ref, tup_ref, tail_up, act_buf):
    b = pl.program_id(0)
    t = pl.program_id(1)
    n_back = FFN_K - 1

    @pl.when(t == 0)
    def _():
        tail_up[...] = jnp.zeros(tail_up.shape, F32)

    x = x_ref[0]
    sh2 = mod_ref[pl.ds(b, 1), 3 * D_MODEL:4 * D_MODEL]
    sc2 = mod_ref[pl.ds(b, 1), 4 * D_MODEL:5 * D_MODEL]
    gt2 = mod_ref[pl.ds(b, 1), 5 * D_MODEL:6 * D_MODEL]
    hb = (_rms(x, gpre_ref[...]) * (1.0 + sc2) + sh2).astype(BF16)

    def conv(lo):
        cols = slice(lo, lo + FF_CHUNK)
        cur = _dot(hb, wu_ref[:, cols])
        ext = jnp.concatenate([_halo_groups(tail_up[:, cols], cur, n_back), cur], axis=0)
        tail_up[:, cols] = cur[TM - SUBLANES * n_back:]
        out = fw_ref[FFN_K - 1:FFN_K, cols] * cur
        for k in range(n_back):
            out = out + fw_ref[k:k + 1, cols] * _back(ext, n_back, n_back - k)
        return out

    for j in range(N_FF_CHUNKS):
        lo = j * FF_CHUNK
        act_buf[:, lo:lo + FF_CHUNK] = (_silu(conv(lo)) * conv(D_FF + lo)).astype(BF16)
    ff = _dot(act_buf[...], wd_ref[...])

    xo_ref[0] = x + gt2 * _rms(ff, gpost_ref[...])

    @pl.when(t == N_SEQ_TILES - 1)
    def _():
        tup_ref[0] = tail_up[...]


def _ffn_prompt_call(l, x, mod, p):
    tile = pl.BlockSpec((1, TM, D_MODEL), lambda b, t: (b, t, 0))
    ls = functools.partial(_layer_spec, l=l)
    once = dict(pipeline_mode=pl.Buffered(1))
    n_back = FFN_K - 1
    return pl.pallas_call(
        _ffn_prompt_kernel,
        grid=(BATCH, N_SEQ_TILES),
        in_specs=[
            tile,
            pl.BlockSpec((None, BATCH, 6 * D_MODEL), lambda b, t: (l, DEC_BATCH // BATCH, 0)),
            ls((1, D_MODEL)), ls((1, D_MODEL)),
            pl.BlockSpec((None, D_MODEL, 2 * D_FF), lambda b, t: (l, 0, 0), **once),
            ls((FFN_K, 2 * D_FF)),
            pl.BlockSpec((None, D_FF, D_MODEL), lambda b, t: (l, 0, 0), **once),
        ],
        out_specs=[tile, pl.BlockSpec((1, SUBLANES * n_back, 2 * D_FF), lambda b, t: (b, 0, 0))],
        out_shape=[
            jax.ShapeDtypeStruct((BATCH, SEQ, D_MODEL), F32),
            jax.ShapeDtypeStruct((BATCH, SUBLANES * n_back, 2 * D_FF), F32),
        ],
        scratch_shapes=[pltpu.VMEM((SUBLANES * n_back, 2 * D_FF), F32),
                        pltpu.VMEM((TM, D_FF), BF16)],
        compiler_params=pltpu.CompilerParams(
            dimension_semantics=("arbitrary", "arbitrary"),
            vmem_limit_bytes=VMEM_LIMIT_BYTES),
        name="ffn_prompt",
    )(x, mod, p["g_ffn_pre"], p["g_ffn_post"], p["w_up"], p["ffn_conv_w"], p["w_down"])


def _slab(a, t):
    return a[t * DEC_BATCH:(t + 1) * DEC_BATCH]


def _mix_dec_kernel(x_ref, mod_ref, gpre_ref, gpost_ref, win_ref, wpool_ref, pscale_ref,
                    pw_ref, wl_ref, sw_ref, cw_ref, cb_ref, lg_ref, lb_ref, wout_ref,
                    spool_ref, sz_ref, sglu_ref,
                    xo_ref, u_ref, z_ref, glu_ref):
    x = x_ref[...]
    rep = lambda a: jnp.concatenate([a] * DEC_SEQ, axis=0)
    sh1 = rep(mod_ref[:, 0:D_MODEL])
    sc1 = rep(mod_ref[:, D_MODEL:2 * D_MODEL])
    gt1 = rep(mod_ref[:, 2 * D_MODEL:3 * D_MODEL])
    h = _rms(x, gpre_ref[...]) * (1.0 + sc1) + sh1
    proj = _dot(h.astype(BF16), win_ref[...])
    u, z_b, bg, glu = _mixer_heads(proj)
    u_ref[...] = u
    z_ref[...] = z_b[(DEC_SEQ - (SCONV_K - 1)) * DEC_BATCH:]
    glu_ref[...] = glu

    def ext(state_ref, cur, n_past, j):
        return state_ref[j] if j < n_past else _slab(cur, j - n_past)

    wins, convs_b, convs_c, cnts = [], [], [], []
    for t in range(DEC_SEQ):
        acc = None
        for i in range(POOL_WINDOWS[3]):
            term = pw_ref[i:i + 1, :] * ext(spool_ref, u, POOL_PAST, POOL_PAST + t - i)
            acc = term if acc is None else acc + term
        wins.append(acc)
        cnts.append(jnp.broadcast_to(jnp.minimum(F32(PAST_LEN + t + 1), wl_ref[...]),
                                     (DEC_BATCH, D_POOL)))
        acc = None
        for k in range(SCONV_K):
            term = sw_ref[k:k + 1, :] * ext(sz_ref, z_b, SCONV_K - 1, t + k)
            acc = term if acc is None else acc + term
        convs_b.append(acc)
        acc = None
        for k in range(CCONV_K):
            term = cw_ref[k:k + 1, :] * ext(sglu_ref, glu, CCONV_K - 1, t + k)
            acc = term if acc is None else acc + term
        convs_c.append(acc)
    cat = lambda xs: jnp.concatenate(xs, axis=0)
    xo_ref[...] = _mixer_tail(x, gt1, gpost_ref[...], cat(wins), cat(cnts), u, cat(convs_b), bg,
                              cat(convs_c), wpool_ref, pscale_ref, cb_ref, lg_ref, lb_ref,
                              wout_ref)


def _full_spec(shape):
    return pl.BlockSpec(shape, lambda i: (0,) * len(shape))


def _mix_dec_call(l, x, mod, spool, sz, sglu, p):
    ls = functools.partial(_layer_spec, l=l)
    return pl.pallas_call(
        _mix_dec_kernel,
        grid=(1,),
        in_specs=[
            _full_spec((DEC_ROWS, D_MODEL)), ls((DEC_BATCH, 6 * D_MODEL)),
            ls((1, D_MODEL)), ls((1, D_MODEL)),
            ls((D_MODEL, D_IN)),
            ls((D_POOL, D_POOL)), ls((1, D_POOL)),
            _full_spec((2 * SUBLANES, D_POOL)), _full_spec((1, D_POOL)),
            ls((SCONV_K, D_SCONV)),
            ls((CCONV_K, D_CCONV)), ls((1, D_CCONV)), ls((1, D_CCONV)), ls((1, D_CCONV)),
            ls((D_MODEL, D_MODEL)),
            ls((POOL_PAST, DEC_BATCH, D_POOL)),
            ls((SCONV_K - 1, DEC_BATCH, D_SCONV)),
            ls((CCONV_K - 1, DEC_BATCH, D_CCONV)),
        ],
        out_specs=[
            _full_spec((DEC_ROWS, D_MODEL)), _full_spec((DEC_ROWS, D_POOL)),
            _full_spec(((SCONV_K - 1) * DEC_BATCH, D_SCONV)), _full_spec((DEC_ROWS, D_CCONV)),
        ],
        out_shape=[
            jax.ShapeDtypeStruct((DEC_ROWS, D_MODEL), F32),
            jax.ShapeDtypeStruct((DEC_ROWS, D_POOL), F32),
            jax.ShapeDtypeStruct(((SCONV_K - 1) * DEC_BATCH, D_SCONV), F32),
            jax.ShapeDtypeStruct((DEC_ROWS, D_CCONV), F32),
        ],
        compiler_params=pltpu.CompilerParams(
            dimension_semantics=("arbitrary",), vmem_limit_bytes=VMEM_LIMIT_BYTES),
        name="mix_dec",
    )(x, mod, p["g_mix_pre"], p["g_mix_post"], p["w_in"], p["wpool"], p["pool_scale"],
      p["pw"], p["wl"], p["sconv_w"], p["cconv_w"], p["cconv_b"], p["cln_g"], p["cln_b"],
      p["w_out"], spool, sz, sglu)


def _ffn_dec_kernel(x_ref, mod_ref, gpre_ref, gpost_ref, wu_ref, fw_ref, wd_ref, sup_ref,
                    xo_ref, up_ref):
    x = x_ref[...]
    rep = lambda a: jnp.concatenate([a] * DEC_SEQ, axis=0)
    sh2 = rep(mod_ref[:, 3 * D_MODEL:4 * D_MODEL])
    sc2 = rep(mod_ref[:, 4 * D_MODEL:5 * D_MODEL])
    gt2 = rep(mod_ref[:, 5 * D_MODEL:6 * D_MODEL])
    hb = (_rms(x, gpre_ref[...]) * (1.0 + sc2) + sh2).astype(BF16)
    n_past = FFN_K - 1

    def conv(lo):
        cols = slice(lo, lo + FF_CHUNK)
        cur = _dot(hb, wu_ref[:, cols])
        up_ref[:, cols] = cur[(DEC_SEQ - n_past) * DEC_BATCH:]
        outs = []
        for t in range(DEC_SEQ):
            acc = None
            for k in range(FFN_K):
                j = t + k
                row = sup_ref[j, :, cols] if j < n_past else _slab(cur, j - n_past)
                term = fw_ref[k:k + 1, cols] * row
                acc = term if acc is None else acc + term
            outs.append(acc)
        return jnp.concatenate(outs, axis=0)

    ff = None
    for j in range(N_FF_CHUNKS):
        act = (_silu(conv(j * FF_CHUNK)) * conv(D_FF + j * FF_CHUNK)).astype(BF16)
        part = _dot(act, wd_ref[j * FF_CHUNK:(j + 1) * FF_CHUNK, :])
        ff = part if ff is None else ff + part
    xo_ref[...] = x + gt2 * _rms(ff, gpost_ref[...])


def _ffn_dec_call(l, x, mod, sup, p):
    n_past = FFN_K - 1
    ls = functools.partial(_layer_spec, l=l)
    return pl.pallas_call(
        _ffn_dec_kernel,
        grid=(1,),
        in_specs=[
            _full_spec((DEC_ROWS, D_MODEL)), ls((DEC_BATCH, 6 * D_MODEL)),
            ls((1, D_MODEL)), ls((1, D_MODEL)),
            ls((D_MODEL, 2 * D_FF)),
            ls((FFN_K, 2 * D_FF)),
            ls((D_FF, D_MODEL)),
            ls((n_past, DEC_BATCH, 2 * D_FF)),
        ],
        out_specs=[_full_spec((DEC_ROWS, D_MODEL)), _full_spec((n_past * DEC_BATCH, 2 * D_FF))],
        out_shape=[
            jax.ShapeDtypeStruct((DEC_ROWS, D_MODEL), F32),
            jax.ShapeDtypeStruct((n_past * DEC_BATCH, 2 * D_FF), F32),
        ],
        compiler_params=pltpu.CompilerParams(
            dimension_semantics=("arbitrary",), vmem_limit_bytes=VMEM_LIMIT_BYTES),
        name="ffn_dec",
    )(x, mod, p["g_ffn_pre"], p["g_ffn_post"], p["w_up"], p["ffn_conv_w"], p["w_down"], sup)


def _block_diag(pool_w):
    g = len(POOL_WINDOWS)
    eye = jnp.eye(g, dtype=bool)[None, :, None, :, None]
    full = jnp.where(eye, pool_w[:, :, :, None, :], 0.0)
    return full.reshape(DEPTH, D_POOL, D_POOL)


def _pool_constants():
    wl = jnp.repeat(jnp.asarray(POOL_WINDOWS, F32), POOL_GC)[None, :]
    pw = (jnp.arange(2 * SUBLANES, dtype=F32)[:, None] < wl).astype(F32)
    return pw, wl


def _tile_permute(x, inverse=False):
    shape = (BATCH, N_SEQ_TILES) + ((SEG, SUBLANES) if inverse else (SUBLANES, SEG)) + (D_MODEL,)
    return x.reshape(shape).swapaxes(2, 3).reshape(BATCH, SEQ, D_MODEL)


def kernel(x_prompt, x_sample, c_prompt, c_sample, state_pool, state_sconv, state_cconv, state_ffn,
           w_ada, b_ada, g_mix_pre, g_mix_post, g_ffn_pre, g_ffn_post, w_in, pool_w, pool_scale,
           sconv_w, cconv_w, cconv_b, cln_g, cln_b, w_out, w_up, ffn_conv_w, w_down):
    mod = _ada_call(jnp.concatenate([c_sample, c_prompt], axis=0), w_ada, b_ada)
    pw, wl = _pool_constants()
    row = lambda a: a[:, None, :]
    p = dict(
        g_mix_pre=row(g_mix_pre), g_mix_post=row(g_mix_post),
        g_ffn_pre=row(g_ffn_pre), g_ffn_post=row(g_ffn_post),
        w_in=w_in.astype(BF16), wpool=_block_diag(pool_w).astype(BF16),
        pool_scale=row(pool_scale), pw=pw, wl=wl,
        sconv_w=sconv_w, cconv_w=cconv_w, cconv_b=row(cconv_b),
        cln_g=row(cln_g), cln_b=row(cln_b),
        w_out=w_out.astype(BF16), w_up=w_up.astype(BF16), ffn_conv_w=ffn_conv_w,
        w_down=w_down.astype(BF16),
    )
    spool, sz, sglu, sup = (jnp.swapaxes(s, 1, 2)
                            for s in (state_pool, state_sconv, state_cconv, state_ffn))

    xp = _tile_permute(x_prompt)
    xs = jnp.swapaxes(x_sample, 0, 1).reshape(DEC_ROWS, D_MODEL)
    tails = [[] for _ in range(4)]
    news = [[] for _ in range(4)]
    for l in range(DEPTH):
        xp, tpool, tz, tglu = _mix_prompt_call(l, xp, mod, p)
        xp, tup = _ffn_prompt_call(l, xp, mod, p)
        for acc, a in zip(tails, (tpool, tz, tglu, tup)):
            acc.append(a)
        xs, u, z, glu = _mix_dec_call(l, xs, mod, spool, sz, sglu, p)
        xs, up = _ffn_dec_call(l, xs, mod, sup, p)
        for acc, a in zip(news, (u, z, glu, up)):
            acc.append(a)

    prompt_state = [jnp.stack(a)[:, :, SUBLANES - 1::SUBLANES] for a in tails]

    def seq_major(parts):
        a = jnp.stack(parts)
        return jnp.swapaxes(a.reshape(DEPTH, -1, DEC_BATCH, a.shape[-1]), 1, 2)

    u, z, glu, up = (seq_major(a) for a in news)
    dec_state = [jnp.concatenate([state_pool[:, :, DEC_SEQ:], u], axis=2), z,
                 jnp.concatenate([state_cconv[:, :, DEC_SEQ:], glu], axis=2), up]
    ys = jnp.swapaxes(xs.reshape(DEC_SEQ, DEC_BATCH, D_MODEL), 0, 1)
    return (_tile_permute(xp, inverse=True), ys) + tuple(prompt_state) + tuple(dec_state)
```

```python
import functools

import jax
import jax.numpy as jnp
from jax import lax
from jax.experimental import pallas as pl
from jax.experimental.pallas import tpu as pltpu

D_MODEL = 1024
BATCH = 8
SEQ = 2048
DEPTH = 4
DEC_BATCH = 128
DEC_SEQ = 4
PAST_LEN = 16384

D_POOL = 256
POOL_WINDOWS = (2, 4, 8, 16)
POOL_GC = 64
POOL_PAST = 15
D_SCONV = 384
D_CCONV = 384
SCONV_K = 3
CCONV_K = 31
FFN_K = 3
D_FF = 2816
D_IN = D_POOL + 3 * D_SCONV + 2 * D_CCONV
EPS = 1e-6

LANES = 128
SUBLANES = 8

O_HB = D_POOL
O_BG = O_HB + D_SCONV
O_CG = O_BG + D_SCONV
O_AC = O_CG + D_SCONV
O_BC = O_AC + D_CCONV
O_GLU = O_AC
N_GLU_TILES = D_CCONV // LANES

FF_CHUNK = 256
N_FF_CHUNKS = D_FF // FF_CHUNK
TM = 1024
N_SEQ_TILES = SEQ // TM
SEG = TM // SUBLANES
DEC_ROWS = DEC_BATCH * DEC_SEQ
N_ADA = BATCH + DEC_BATCH
ADA_TN = 1024
VMEM_LIMIT_BYTES = 56 * 1024 * 1024

F32 = jnp.float32
BF16 = jnp.bfloat16

assert 2 * POOL_GC == LANES and len(POOL_WINDOWS) == 4
assert SEG >= CCONV_K - 1


def _rms_scaled(x, scale):
    return x * lax.rsqrt(jnp.mean(x * x, axis=-1, keepdims=True) + EPS) * scale


def _layer_norm(x, g, b):
    mu = jnp.mean(x, axis=-1, keepdims=True)
    xc = x - mu
    var = jnp.mean(xc * xc, axis=-1, keepdims=True)
    return xc * lax.rsqrt(var + EPS) * g + b


def _silu(x):
    return x * jax.nn.sigmoid(x)


def _dot(a, b):
    return jnp.dot(a, b, preferred_element_type=F32)


def _layer_spec(shape, l, **kw):
    return pl.BlockSpec((None,) + shape, lambda *_: (l,) + (0,) * len(shape), **kw)


def _ada_kernel(c_ref, w_ref, b_ref, o_ref):
    s = _silu(c_ref[...]).astype(BF16)
    o_ref[...] = _dot(s, w_ref[...].astype(BF16)) + b_ref[...]


def _ada_call(c_all, w_ada, b_ada):
    return pl.pallas_call(
        _ada_kernel,
        grid=(DEPTH, 6 * D_MODEL // ADA_TN),
        in_specs=[
            pl.BlockSpec((N_ADA, D_MODEL), lambda l, n: (0, 0)),
            pl.BlockSpec((None, D_MODEL, ADA_TN), lambda l, n: (l, 0, n)),
            pl.BlockSpec((None, 1, ADA_TN), lambda l, n: (l, 0, n)),
        ],
        out_specs=pl.BlockSpec((None, N_ADA, ADA_TN), lambda l, n: (l, 0, n)),
        out_shape=jax.ShapeDtypeStruct((DEPTH, N_ADA, 6 * D_MODEL), F32),
        compiler_params=pltpu.CompilerParams(
            dimension_semantics=("arbitrary", "arbitrary"),
            vmem_limit_bytes=VMEM_LIMIT_BYTES),
        name="ada",
    )(c_all, w_ada, b_ada.reshape(DEPTH, 1, 6 * D_MODEL))


def _mixer_heads(proj):
    u = proj[:, 0:O_HB]
    z_b = proj[:, O_CG:O_AC] * proj[:, O_HB:O_BG]
    bg = proj[:, O_BG:O_CG]
    glu = []
    for q in range(N_GLU_TILES):
        a = proj[:, O_GLU + 2 * q * LANES:O_GLU + (2 * q + 1) * LANES]
        b = proj[:, O_GLU + (2 * q + 1) * LANES:O_GLU + (2 * q + 2) * LANES]
        glu.append(a * jax.nn.sigmoid(b))
    return u, z_b, bg, jnp.concatenate(glu, axis=-1)


def _pool_windows(tap):
    halves = []
    for h in range(2):
        w_small, w_big = POOL_WINDOWS[2 * h], POOL_WINDOWS[2 * h + 1]
        cols = slice(h * LANES, (h + 1) * LANES)
        s = tap(0, cols)
        for i in range(1, w_small):
            s = s + tap(i, cols)
        small = s
        for i in range(w_small, w_big):
            s = s + tap(i, cols)
        lane = lax.broadcasted_iota(jnp.int32, s.shape, 1)
        halves.append(jnp.where(lane < POOL_GC, small, s))
    return jnp.concatenate(halves, axis=-1)


def _mixer_out(x, gt1, gpost_ref, win, cnt, u, conv_b, bg, conv_c,
               wpool_ref, pscale_ref, cb_ref, lg_ref, lb_ref, wout_ref):
    pooled = (win / cnt - u).astype(BF16)
    y_a = (_dot(pooled, wpool_ref[...]) * pscale_ref[...]).astype(BF16)
    y_b = (bg * conv_b).astype(BF16)
    y_c = _silu(_layer_norm(conv_c + cb_ref[...], lg_ref[...], lb_ref[...])).astype(BF16)
    mix = _dot(jnp.concatenate([y_a, y_b, y_c], axis=-1), wout_ref[...])
    return x + _rms_scaled(mix, gt1 * gpost_ref[...])


def _halo_groups(prev_tail, cur, n_back):
    tail = cur[TM - SUBLANES * n_back:]
    sub = lax.broadcasted_iota(jnp.int32, tail.shape, 0) & (SUBLANES - 1)
    mixed = jnp.where(sub == SUBLANES - 1, prev_tail, tail)
    groups = [pltpu.roll(mixed[SUBLANES * i:SUBLANES * (i + 1)], 1, axis=0) for i in range(n_back)]
    return jnp.concatenate(groups, axis=0)


def _back(ext, n_back, j):
    lo = SUBLANES * (n_back - j)
    return ext[lo:lo + TM]


def _mix_prompt_kernel(x_ref, mod_ref, gpre_ref, gpost_ref, win_ref, wpool_ref, pscale_ref,
                       wl_ref, sw_ref, cw_ref, cb_ref, lg_ref, lb_ref, wout_ref,
                       xo_ref, tpool_ref, tz_ref, tglu_ref,
                       tail_u, tail_z, tail_glu):
    b = pl.program_id(0)
    t = pl.program_id(1)

    @pl.when(t == 0)
    def _():
        tail_u[...] = jnp.zeros(tail_u.shape, F32)
        tail_z[...] = jnp.zeros(tail_z.shape, F32)
        tail_glu[...] = jnp.zeros(tail_glu.shape, F32)

    x = x_ref[0]
    sh1 = mod_ref[pl.ds(b, 1), 0:D_MODEL]
    sc1 = mod_ref[pl.ds(b, 1), D_MODEL:2 * D_MODEL]
    gt1 = mod_ref[pl.ds(b, 1), 2 * D_MODEL:3 * D_MODEL]
    h = _rms_scaled(x, gpre_ref[...] * (1.0 + sc1)) + sh1
    u, z_b, bg, glu = _mixer_heads(_dot(h.astype(BF16), win_ref[...]))

    eu = jnp.concatenate([_halo_groups(tail_u[...], u, POOL_PAST), u], axis=0)
    ez = jnp.concatenate([_halo_groups(tail_z[...], z_b, SCONV_K - 1), z_b], axis=0)
    eg = jnp.concatenate([_halo_groups(tail_glu[...], glu, CCONV_K - 1), glu], axis=0)
    tail_u[...] = u[TM - SUBLANES * POOL_PAST:]
    tail_z[...] = z_b[TM - SUBLANES * (SCONV_K - 1):]
    tail_glu[...] = glu[TM - SUBLANES * (CCONV_K - 1):]

    win = _pool_windows(lambda i, cols: _back(eu, POOL_PAST, i)[:, cols])
    row = lax.broadcasted_iota(jnp.int32, (TM, D_POOL), 0)
    pos1 = t * TM + 1 + (row & (SUBLANES - 1)) * SEG + (row >> 3)
    cnt = jnp.minimum(pos1.astype(F32), wl_ref[...])

    conv_b = sw_ref[SCONV_K - 1:SCONV_K, :] * z_b
    for k in range(SCONV_K - 1):
        conv_b = conv_b + sw_ref[k:k + 1, :] * _back(ez, SCONV_K - 1, SCONV_K - 1 - k)
    conv_c = cw_ref[CCONV_K - 1:CCONV_K, :] * glu
    for k in range(CCONV_K - 1):
        conv_c = conv_c + cw_ref[k:k + 1, :] * _back(eg, CCONV_K - 1, CCONV_K - 1 - k)

    xo_ref[0] = _mixer_out(x, gt1, gpost_ref, win, cnt, u, conv_b, bg, conv_c,
                           wpool_ref, pscale_ref, cb_ref, lg_ref, lb_ref, wout_ref)

    @pl.when(t == N_SEQ_TILES - 1)
    def _():
        tpool_ref[0] = tail_u[...]
        tz_ref[0] = tail_z[...]
        tglu_ref[0] = tail_glu[...]


def _mix_prompt_call(l, x, mod, p):
    tile = pl.BlockSpec((1, TM, D_MODEL), lambda b, t: (b, t, 0))
    tail = lambda n_back, ch: pl.BlockSpec((1, SUBLANES * n_back, ch), lambda b, t: (b, 0, 0))
    tail_shape = lambda n_back, ch: (BATCH, SUBLANES * n_back, ch)
    once = dict(pipeline_mode=pl.Buffered(1))
    ls = functools.partial(_layer_spec, l=l)
    return pl.pallas_call(
        _mix_prompt_kernel,
        grid=(BATCH, N_SEQ_TILES),
        in_specs=[
            tile,
            pl.BlockSpec((None, BATCH, 6 * D_MODEL), lambda b, t: (l, DEC_BATCH // BATCH, 0)),
            ls((1, D_MODEL)), ls((1, D_MODEL)),
            ls((D_MODEL, D_IN), **once),
            ls((D_POOL, D_POOL)), ls((1, D_POOL)),
            pl.BlockSpec((1, D_POOL), lambda b, t: (0, 0)),
            ls((SCONV_K, D_SCONV)),
            ls((CCONV_K, D_CCONV)), ls((1, D_CCONV)), ls((1, D_CCONV)), ls((1, D_CCONV)),
            ls((D_MODEL, D_MODEL), **once),
        ],
        out_specs=[tile, tail(POOL_PAST, D_POOL), tail(SCONV_K - 1, D_SCONV),
                   tail(CCONV_K - 1, D_CCONV)],
        out_shape=[
            jax.ShapeDtypeStruct((BATCH, SEQ, D_MODEL), F32),
            jax.ShapeDtypeStruct(tail_shape(POOL_PAST, D_POOL), F32),
            jax.ShapeDtypeStruct(tail_shape(SCONV_K - 1, D_SCONV), F32),
            jax.ShapeDtypeStruct(tail_shape(CCONV_K - 1, D_CCONV), F32),
        ],
        scratch_shapes=[
            pltpu.VMEM((SUBLANES * POOL_PAST, D_POOL), F32),
            pltpu.VMEM((SUBLANES * (SCONV_K - 1), D_SCONV), F32),
            pltpu.VMEM((SUBLANES * (CCONV_K - 1), D_CCONV), F32),
        ],
        compiler_params=pltpu.CompilerParams(
            dimension_semantics=("arbitrary", "arbitrary"),
            vmem_limit_bytes=VMEM_LIMIT_BYTES),
        name="mix_prompt",
    )(x, mod, p["g_mix_pre"], p["g_mix_post"], p["w_in"], p["wpool"], p["pool_scale"],
      p["wl"], p["sconv_w"], p["cconv_w"], p["cconv_b"], p["cln_g"], p["cln_b"], p["w_out"])


def _ffn_prompt_kernel(x_ref, mod_ref, gpre_ref, gpost_ref, wu_ref, fw_ref, wd_ref,
                       xo_ref, tup_ref, tail_up, act_buf):
    b = pl.program_id(0)
    t = pl.program_id(1)
    n_back = FFN_K - 1

    @pl.when(t == 0)
    def _():
        tail_up[...] = jnp.zeros(tail_up.shape, F32)

    x = x_ref[0]
    sh2 = mod_ref[pl.ds(b, 1), 3 * D_MODEL:4 * D_MODEL]
    sc2 = mod_ref[pl.ds(b, 1), 4 * D_MODEL:5 * D_MODEL]
    gt2 = mod_ref[pl.ds(b, 1), 5 * D_MODEL:6 * D_MODEL]
    hb = (_rms_scaled(x, gpre_ref[...] * (1.0 + sc2)) + sh2).astype(BF16)

    def conv(lo):
        cols = slice(lo, lo + FF_CHUNK)
        cur = _dot(hb, wu_ref[:, cols])
        ext = jnp.concatenate([_halo_groups(tail_up[:, cols], cur, n_back), cur], axis=0)
        tail_up[:, cols] = cur[TM - SUBLANES * n_back:]
        out = fw_ref[FFN_K - 1:FFN_K, cols] * cur
        for k in range(n_back):
            out = out + fw_ref[k:k + 1, cols] * _back(ext, n_back, n_back - k)
        return out

    for j in range(N_FF_CHUNKS):
        lo = j * FF_CHUNK
        act_buf[:, lo:lo + FF_CHUNK] = (_silu(conv(lo)) * conv(D_FF + lo)).astype(BF16)
    ff = _dot(act_buf[...], wd_ref[...])

    xo_ref[0] = x + _rms_scaled(ff, gt2 * gpost_ref[...])

    @pl.when(t == N_SEQ_TILES - 1)
    def _():
        tup_ref[0] = tail_up[...]


def _ffn_prompt_call(l, x, mod, p):
    tile = pl.BlockSpec((1, TM, D_MODEL), lambda b, t: (b, t, 0))
    ls = functools.partial(_layer_spec, l=l)
    once = dict(pipeline_mode=pl.Buffered(1))
    n_back = FFN_K - 1
    return pl.pallas_call(
        _ffn_prompt_kernel,
        grid=(BATCH, N_SEQ_TILES),
        in_specs=[
            tile,
            pl.BlockSpec((None, BATCH, 6 * D_MODEL), lambda b, t: (l, DEC_BATCH // BATCH, 0)),
            ls((1, D_MODEL)), ls((1, D_MODEL)),
            ls((D_MODEL, 2 * D_FF), **once),
            ls((FFN_K, 2 * D_FF)),
            ls((D_FF, D_MODEL), **once),
        ],
        out_specs=[tile, pl.BlockSpec((1, SUBLANES * n_back, 2 * D_FF), lambda b, t: (b, 0, 0))],
        out_shape=[
            jax.ShapeDtypeStruct((BATCH, SEQ, D_MODEL), F32),
            jax.ShapeDtypeStruct((BATCH, SUBLANES * n_back, 2 * D_FF), F32),
        ],
        scratch_shapes=[pltpu.VMEM((SUBLANES * n_back, 2 * D_FF), F32),
                        pltpu.VMEM((TM, D_FF), BF16)],
        compiler_params=pltpu.CompilerParams(
            dimension_semantics=("arbitrary", "arbitrary"),
            vmem_limit_bytes=VMEM_LIMIT_BYTES),
        name="ffn_prompt",
    )(x, mod, p["g_ffn_pre"], p["g_ffn_post"], p["w_up"], p["ffn_conv_w"], p["w_down"])


def _slab(a, t):
    return a[t * DEC_BATCH:(t + 1) * DEC_BATCH]


def _mix_dec_kernel(x_ref, mod_ref, gpre_ref, gpost_ref, win_ref, wpool_ref, pscale_ref,
                    wl_ref, sw_ref, cw_ref, cb_ref, lg_ref, lb_ref, wout_ref,
                    spool_ref, sz_ref, sglu_ref,
                    xo_ref, npool_ref, nz_ref, nglu_ref):
    x = x_ref[...]
    rep = lambda a: jnp.concatenate([a] * DEC_SEQ, axis=0)
    sh1 = rep(mod_ref[:, 0:D_MODEL])
    sc1 = rep(mod_ref[:, D_MODEL:2 * D_MODEL])
    gt1 = rep(mod_ref[:, 2 * D_MODEL:3 * D_MODEL])
    h = _rms_scaled(x, gpre_ref[...] * (1.0 + sc1)) + sh1
    u, z_b, bg, glu = _mixer_heads(_dot(h.astype(BF16), win_ref[...]))

    def ext(state_ref, cur, n_past, j):
        return state_ref[j] if j < n_past else _slab(cur, j - n_past)

    for new_ref, state_ref, cur in ((npool_ref, spool_ref, u), (nz_ref, sz_ref, z_b),
                                    (nglu_ref, sglu_ref, glu)):
        n_past = state_ref.shape[0]
        for k in range(n_past):
            new_ref[k] = ext(state_ref, cur, n_past, k + DEC_SEQ)

    wins, convs_b, convs_c, cnts = [], [], [], []
    for t in range(DEC_SEQ):
        wins.append(_pool_windows(
            lambda i, cols: ext(spool_ref, u, POOL_PAST, POOL_PAST + t - i)[:, cols]))
        cnts.append(jnp.broadcast_to(jnp.minimum(F32(PAST_LEN + t + 1), wl_ref[...]),
                                     (DEC_BATCH, D_POOL)))
        acc = None
        for k in range(SCONV_K):
            term = sw_ref[k:k + 1, :] * ext(sz_ref, z_b, SCONV_K - 1, t + k)
            acc = term if acc is None else acc + term
        convs_b.append(acc)
        acc = None
        for k in range(CCONV_K):
            term = cw_ref[k:k + 1, :] * ext(sglu_ref, glu, CCONV_K - 1, t + k)
            acc = term if acc is None else acc + term
        convs_c.append(acc)
    cat = lambda xs: jnp.concatenate(xs, axis=0)
    xo_ref[...] = _mixer_out(x, gt1, gpost_ref, cat(wins), cat(cnts), u, cat(convs_b), bg,
                             cat(convs_c), wpool_ref, pscale_ref, cb_ref, lg_ref, lb_ref,
                             wout_ref)


def _full_spec(shape):
    return pl.BlockSpec(shape, lambda i: (0,) * len(shape))


def _mix_dec_call(l, x, mod, spool, sz, sglu, p):
    ls = functools.partial(_layer_spec, l=l)
    states = (spool, sz, sglu)
    return pl.pallas_call(
        _mix_dec_kernel,
        grid=(1,),
        in_specs=[
            _full_spec((DEC_ROWS, D_MODEL)), ls((DEC_BATCH, 6 * D_MODEL)),
            ls((1, D_MODEL)), ls((1, D_MODEL)),
            ls((D_MODEL, D_IN)),
            ls((D_POOL, D_POOL)), ls((1, D_POOL)),
            _full_spec((1, D_POOL)),
            ls((SCONV_K, D_SCONV)),
            ls((CCONV_K, D_CCONV)), ls((1, D_CCONV)), ls((1, D_CCONV)), ls((1, D_CCONV)),
            ls((D_MODEL, D_MODEL)),
        ] + [ls(s.shape[1:]) for s in states],
        out_specs=[_full_spec((DEC_ROWS, D_MODEL))] + [_full_spec(s.shape[1:]) for s in states],
        out_shape=[jax.ShapeDtypeStruct((DEC_ROWS, D_MODEL), F32)]
        + [jax.ShapeDtypeStruct(s.shape[1:], F32) for s in states],
        compiler_params=pltpu.CompilerParams(
            dimension_semantics=("arbitrary",), vmem_limit_bytes=VMEM_LIMIT_BYTES),
        name="mix_dec",
    )(x, mod, p["g_mix_pre"], p["g_mix_post"], p["w_in"], p["wpool"], p["pool_scale"],
      p["wl"], p["sconv_w"], p["cconv_w"], p["cconv_b"], p["cln_g"], p["cln_b"],
      p["w_out"], *states)


def _ffn_dec_kernel(x_ref, mod_ref, gpre_ref, gpost_ref, wu_ref, fw_ref, wd_ref, sup_ref,
                    xo_ref, up_ref, act_buf):
    x = x_ref[...]
    rep = lambda a: jnp.concatenate([a] * DEC_SEQ, axis=0)
    sh2 = rep(mod_ref[:, 3 * D_MODEL:4 * D_MODEL])
    sc2 = rep(mod_ref[:, 4 * D_MODEL:5 * D_MODEL])
    gt2 = rep(mod_ref[:, 5 * D_MODEL:6 * D_MODEL])
    hb = (_rms_scaled(x, gpre_ref[...] * (1.0 + sc2)) + sh2).astype(BF16)
    n_past = FFN_K - 1

    def conv(lo):
        cols = slice(lo, lo + FF_CHUNK)
        cur = _dot(hb, wu_ref[:, cols])
        for k in range(n_past):
            up_ref[k, :, cols] = _slab(cur, DEC_SEQ - n_past + k)
        outs = []
        for t in range(DEC_SEQ):
            acc = None
            for k in range(FFN_K):
                j = t + k
                row = sup_ref[j, :, cols] if j < n_past else _slab(cur, j - n_past)
                term = fw_ref[k:k + 1, cols] * row
                acc = term if acc is None else acc + term
            outs.append(acc)
        return jnp.concatenate(outs, axis=0)

    for j in range(N_FF_CHUNKS):
        lo = j * FF_CHUNK
        act_buf[:, lo:lo + FF_CHUNK] = (_silu(conv(lo)) * conv(D_FF + lo)).astype(BF16)
    ff = _dot(act_buf[...], wd_ref[...])
    xo_ref[...] = x + _rms_scaled(ff, gt2 * gpost_ref[...])


def _ffn_dec_call(l, x, mod, sup, p):
    ls = functools.partial(_layer_spec, l=l)
    return pl.pallas_call(
        _ffn_dec_kernel,
        grid=(1,),
        in_specs=[
            _full_spec((DEC_ROWS, D_MODEL)), ls((DEC_BATCH, 6 * D_MODEL)),
            ls((1, D_MODEL)), ls((1, D_MODEL)),
            ls((D_MODEL, 2 * D_FF)),
            ls((FFN_K, 2 * D_FF)),
            ls((D_FF, D_MODEL)),
            ls(sup.shape[1:]),
        ],
        out_specs=[_full_spec((DEC_ROWS, D_MODEL)), _full_spec(sup.shape[1:])],
        out_shape=[
            jax.ShapeDtypeStruct((DEC_ROWS, D_MODEL), F32),
            jax.ShapeDtypeStruct(sup.shape[1:], F32),
        ],
        scratch_shapes=[pltpu.VMEM((DEC_ROWS, D_FF), BF16)],
        compiler_params=pltpu.CompilerParams(
            dimension_semantics=("arbitrary",), vmem_limit_bytes=VMEM_LIMIT_BYTES),
        name="ffn_dec",
    )(x, mod, p["g_ffn_pre"], p["g_ffn_post"], p["w_up"], p["ffn_conv_w"], p["w_down"], sup)


def _block_diag(pool_w):
    g = len(POOL_WINDOWS)
    eye = jnp.eye(g, dtype=bool)[None, :, None, :, None]
    full = jnp.where(eye, pool_w[:, :, :, None, :], 0.0)
    return full.reshape(DEPTH, D_POOL, D_POOL)


def _interleave_glu_columns(w_in):
    parts = [w_in[..., :O_AC]]
    for q in range(N_GLU_TILES):
        parts.append(w_in[..., O_AC + q * LANES:O_AC + (q + 1) * LANES])
        parts.append(w_in[..., O_BC + q * LANES:O_BC + (q + 1) * LANES])
    return jnp.concatenate(parts, axis=-1)


def _tile_permute(x, inverse=False):
    shape = (BATCH, N_SEQ_TILES) + ((SEG, SUBLANES) if inverse else (SUBLANES, SEG)) + (D_MODEL,)
    return x.reshape(shape).swapaxes(2, 3).reshape(BATCH, SEQ, D_MODEL)


def kernel(x_prompt, x_sample, c_prompt, c_sample, state_pool, state_sconv, state_cconv, state_ffn,
           w_ada, b_ada, g_mix_pre, g_mix_post, g_ffn_pre, g_ffn_post, w_in, pool_w, pool_scale,
           sconv_w, cconv_w, cconv_b, cln_g, cln_b, w_out, w_up, ffn_conv_w, w_down):
    mod = _ada_call(jnp.concatenate([c_sample, c_prompt], axis=0), w_ada, b_ada)
    row = lambda a: a[:, None, :]
    p = dict(
        g_mix_pre=row(g_mix_pre), g_mix_post=row(g_mix_post),
        g_ffn_pre=row(g_ffn_pre), g_ffn_post=row(g_ffn_post),
        w_in=_interleave_glu_columns(w_in).astype(BF16),
        wpool=_block_diag(pool_w).astype(BF16),
        pool_scale=row(pool_scale),
        wl=jnp.repeat(jnp.asarray(POOL_WINDOWS, F32), POOL_GC)[None, :],
        sconv_w=sconv_w, cconv_w=cconv_w, cconv_b=row(cconv_b),
        cln_g=row(cln_g), cln_b=row(cln_b),
        w_out=w_out.astype(BF16), w_up=w_up.astype(BF16), ffn_conv_w=ffn_conv_w,
        w_down=w_down.astype(BF16),
    )
    spool, sz, sglu, sup = (jnp.swapaxes(s, 1, 2)
                            for s in (state_pool, state_sconv, state_cconv, state_ffn))

    xp = _tile_permute(x_prompt)
    xs = jnp.swapaxes(x_sample, 0, 1).reshape(DEC_ROWS, D_MODEL)
    tails = [[] for _ in range(4)]
    news = [[] for _ in range(4)]
    for l in range(DEPTH):
        xp, tpool, tz, tglu = _mix_prompt_call(l, xp, mod, p)
        xp, tup = _ffn_prompt_call(l, xp, mod, p)
        for acc, a in zip(tails, (tpool, tz, tglu, tup)):
            acc.append(a)
        xs, npool, nz, nglu = _mix_dec_call(l, xs, mod, spool, sz, sglu, p)
        xs, nup = _ffn_dec_call(l, xs, mod, sup, p)
        for acc, a in zip(news, (npool, nz, nglu, nup)):
            acc.append(a)

    prompt_state = [jnp.stack(a)[:, :, SUBLANES - 1::SUBLANES] for a in tails]
    dec_state = [jnp.swapaxes(jnp.stack(a), 1, 2) for a in news]
    ys = jnp.swapaxes(xs.reshape(DEC_SEQ, DEC_BATCH, D_MODEL), 0, 1)
    return (_tile_permute(xp, inverse=True), ys) + tuple(prompt_state) + tuple(dec_state)
```

```python
import functools

import jax
import jax.numpy as jnp
from jax import lax
from jax.experimental import pallas as pl
from jax.experimental.pallas import tpu as pltpu

D_MODEL = 1024
BATCH = 8
SEQ = 2048
DEPTH = 4
DEC_BATCH = 128
DEC_SEQ = 4
PAST_LEN = 16384

D_POOL = 256
POOL_WINDOWS = (2, 4, 8, 16)
POOL_GC = 64
POOL_PAST = 15
D_SCONV = 384
D_CCONV = 384
SCONV_K = 3
CCONV_K = 31
FFN_K = 3
D_FF = 2816
D_IN = D_POOL + 3 * D_SCONV + 2 * D_CCONV
EPS = 1e-6

LANES = 128
SUBLANES = 8

O_HB = D_POOL
O_BG = O_HB + D_SCONV
O_CG = O_BG + D_SCONV
O_AC = O_CG + D_SCONV
O_BC = O_AC + D_CCONV
O_GLU = O_AC
N_GLU_TILES = D_CCONV // LANES

FF_CHUNK = 256
N_FF_CHUNKS = D_FF // FF_CHUNK
TM = 1024
N_SEQ_TILES = SEQ // TM
SEG = TM // SUBLANES
DEC_ROWS = DEC_BATCH * DEC_SEQ
N_ADA = BATCH + DEC_BATCH
ADA_TN = 3072
VMEM_LIMIT_BYTES = 56 * 1024 * 1024

F32 = jnp.float32
BF16 = jnp.bfloat16

assert 2 * POOL_GC == LANES and len(POOL_WINDOWS) == 4
assert SEG >= CCONV_K - 1


def _rms_scaled(x, scale):
    return x * lax.rsqrt(jnp.mean(x * x, axis=-1, keepdims=True) + EPS) * scale


def _layer_norm(x, g, b):
    mu = jnp.mean(x, axis=-1, keepdims=True)
    xc = x - mu
    var = jnp.mean(xc * xc, axis=-1, keepdims=True)
    return xc * lax.rsqrt(var + EPS) * g + b


def _silu(x):
    return x * jax.nn.sigmoid(x)


def _dot(a, b):
    return jnp.dot(a, b, preferred_element_type=F32)


def _layer_spec(shape, l, **kw):
    return pl.BlockSpec((None,) + shape, lambda *_: (l,) + (0,) * len(shape), **kw)


def _whole_spec(shape, **kw):
    return pl.BlockSpec(shape, lambda *_: (0,) * len(shape), **kw)


def _ada_kernel(c_ref, w_ref, b_ref, o_ref):
    s = _silu(c_ref[...]).astype(BF16)
    o_ref[...] = _dot(s, w_ref[...].astype(BF16)) + b_ref[...]


def _ada_call(c_all, w_ada, b_ada):
    return pl.pallas_call(
        _ada_kernel,
        grid=(DEPTH, 6 * D_MODEL // ADA_TN),
        in_specs=[
            pl.BlockSpec((N_ADA, D_MODEL), lambda l, n: (0, 0)),
            pl.BlockSpec((None, D_MODEL, ADA_TN), lambda l, n: (l, 0, n)),
            pl.BlockSpec((None, 1, ADA_TN), lambda l, n: (l, 0, n)),
        ],
        out_specs=pl.BlockSpec((None, N_ADA, ADA_TN), lambda l, n: (l, 0, n)),
        out_shape=jax.ShapeDtypeStruct((DEPTH, N_ADA, 6 * D_MODEL), F32),
        compiler_params=pltpu.CompilerParams(
            dimension_semantics=("arbitrary", "arbitrary"),
            vmem_limit_bytes=VMEM_LIMIT_BYTES),
        name="ada",
    )(c_all, w_ada, b_ada.reshape(DEPTH, 1, 6 * D_MODEL))


def _mixer_heads(proj):
    u = proj[:, 0:O_HB]
    z_b = proj[:, O_CG:O_AC] * proj[:, O_HB:O_BG]
    bg = proj[:, O_BG:O_CG]
    glu = []
    for q in range(N_GLU_TILES):
        a = proj[:, O_GLU + 2 * q * LANES:O_GLU + (2 * q + 1) * LANES]
        b = proj[:, O_GLU + (2 * q + 1) * LANES:O_GLU + (2 * q + 2) * LANES]
        glu.append(a * jax.nn.sigmoid(b))
    return u, z_b, bg, jnp.concatenate(glu, axis=-1)


def _pool_windows(tap):
    halves = []
    for h in range(2):
        w_small, w_big = POOL_WINDOWS[2 * h], POOL_WINDOWS[2 * h + 1]
        cols = slice(h * LANES, (h + 1) * LANES)
        s = tap(0, cols)
        for i in range(1, w_small):
            s = s + tap(i, cols)
        small = s
        for i in range(w_small, w_big):
            s = s + tap(i, cols)
        lane = lax.broadcasted_iota(jnp.int32, s.shape, 1)
        halves.append(jnp.where(lane < POOL_GC, small, s))
    return jnp.concatenate(halves, axis=-1)


def _mixer_out(x, gt1, gpost_ref, win, cnt, u, conv_b, bg, conv_c,
               wpool_ref, pscale_ref, cb_ref, lg_ref, lb_ref, wout_ref):
    pooled = (win / cnt - u).astype(BF16)
    y_a = (_dot(pooled, wpool_ref[...]) * pscale_ref[...]).astype(BF16)
    y_b = (bg * conv_b).astype(BF16)
    y_c = _silu(_layer_norm(conv_c + cb_ref[...], lg_ref[...], lb_ref[...])).astype(BF16)
    mix = _dot(jnp.concatenate([y_a, y_b, y_c], axis=-1), wout_ref[...])
    return x + _rms_scaled(mix, gt1 * gpost_ref[...])


def _halo_groups(prev_tail, cur, n_back):
    tail = cur[TM - SUBLANES * n_back:]
    sub = lax.broadcasted_iota(jnp.int32, tail.shape, 0) & (SUBLANES - 1)
    mixed = jnp.where(sub == SUBLANES - 1, prev_tail, tail)
    groups = [pltpu.roll(mixed[SUBLANES * i:SUBLANES * (i + 1)], 1, axis=0) for i in range(n_back)]
    return jnp.concatenate(groups, axis=0)


def _back(ext, n_back, j):
    lo = SUBLANES * (n_back - j)
    return ext[lo:lo + TM]


def _mix_prompt_kernel(x_ref, mod_ref, gpre_ref, gpost_ref, win_ref, wpool_ref, pscale_ref,
                       wl_ref, sw_ref, cw_ref, cb_ref, lg_ref, lb_ref, wout_ref, *rest,
                       cast_next):
    b = pl.program_id(0)
    t = pl.program_id(1)
    if cast_next:
        next_f32, rest = rest[:4], rest[4:]
        xo_ref, tpool_ref, tz_ref, tglu_ref = rest[:4]
        next_bf16, rest = rest[4:8], rest[8:]
        next_bf16[0][...] = _interleave_glu_columns(next_f32[0][...]).astype(BF16)
        for src, dst in zip(next_f32[1:], next_bf16[1:]):
            dst[...] = src[...].astype(BF16)
    else:
        xo_ref, tpool_ref, tz_ref, tglu_ref = rest[:4]
        rest = rest[4:]
    tail_u, tail_z, tail_glu = rest

    @pl.when(t == 0)
    def _():
        tail_u[...] = jnp.zeros(tail_u.shape, F32)
        tail_z[...] = jnp.zeros(tail_z.shape, F32)
        tail_glu[...] = jnp.zeros(tail_glu.shape, F32)

    x = x_ref[0]
    sh1 = mod_ref[pl.ds(b, 1), 0:D_MODEL]
    sc1 = mod_ref[pl.ds(b, 1), D_MODEL:2 * D_MODEL]
    gt1 = mod_ref[pl.ds(b, 1), 2 * D_MODEL:3 * D_MODEL]
    h = _rms_scaled(x, gpre_ref[...] * (1.0 + sc1)) + sh1
    u, z_b, bg, glu = _mixer_heads(_dot(h.astype(BF16), win_ref[...]))

    eu = jnp.concatenate([_halo_groups(tail_u[...], u, POOL_PAST), u], axis=0)
    ez = jnp.concatenate([_halo_groups(tail_z[...], z_b, SCONV_K - 1), z_b], axis=0)
    eg = jnp.concatenate([_halo_groups(tail_glu[...], glu, CCONV_K - 1), glu], axis=0)
    tail_u[...] = u[TM - SUBLANES * POOL_PAST:]
    tail_z[...] = z_b[TM - SUBLANES * (SCONV_K - 1):]
    tail_glu[...] = glu[TM - SUBLANES * (CCONV_K - 1):]

    win = _pool_windows(lambda i, cols: _back(eu, POOL_PAST, i)[:, cols])
    row = lax.broadcasted_iota(jnp.int32, (TM, D_POOL), 0)
    pos1 = t * TM + 1 + (row & (SUBLANES - 1)) * SEG + (row >> 3)
    cnt = jnp.minimum(pos1.astype(F32), wl_ref[...])

    conv_b = sw_ref[SCONV_K - 1:SCONV_K, :] * z_b
    for k in range(SCONV_K - 1):
        conv_b = conv_b + sw_ref[k:k + 1, :] * _back(ez, SCONV_K - 1, SCONV_K - 1 - k)
    conv_c = cw_ref[CCONV_K - 1:CCONV_K, :] * glu
    for k in range(CCONV_K - 1):
        conv_c = conv_c + cw_ref[k:k + 1, :] * _back(eg, CCONV_K - 1, CCONV_K - 1 - k)

    xo_ref[0] = _mixer_out(x, gt1, gpost_ref, win, cnt, u, conv_b, bg, conv_c,
                           wpool_ref, pscale_ref, cb_ref, lg_ref, lb_ref, wout_ref)

    @pl.when(t == N_SEQ_TILES - 1)
    def _():
        tpool_ref[0] = tail_u[...]
        tz_ref[0] = tail_z[...]
        tglu_ref[0] = tail_glu[...]


def _mix_prompt_call(l, x, mod, p, w, next_f32):
    tile = pl.BlockSpec((1, TM, D_MODEL), lambda b, t: (b, t, 0))
    tail = lambda n_back, ch: pl.BlockSpec((1, SUBLANES * n_back, ch), lambda b, t: (b, 0, 0))
    tail_shape = lambda n_back, ch: (BATCH, SUBLANES * n_back, ch)
    once = dict(pipeline_mode=pl.Buffered(1))
    ls = functools.partial(_layer_spec, l=l)
    n_steps = BATCH * N_SEQ_TILES
    cast_in, cast_out, cast_shape = [], [], []
    for a in next_f32 or ():
        rows, cols = a.shape[1] // n_steps, a.shape[2]
        cast_in.append(pl.BlockSpec((None, rows, cols),
                                    lambda b, t: (l + 1, b * N_SEQ_TILES + t, 0)))
        cast_out.append(pl.BlockSpec((rows, cols), lambda b, t: (b * N_SEQ_TILES + t, 0)))
        cast_shape.append(jax.ShapeDtypeStruct(a.shape[1:], BF16))
    return pl.pallas_call(
        functools.partial(_mix_prompt_kernel, cast_next=bool(next_f32)),
        grid=(BATCH, N_SEQ_TILES),
        in_specs=[
            tile,
            pl.BlockSpec((None, BATCH, 6 * D_MODEL), lambda b, t: (l, DEC_BATCH // BATCH, 0)),
            ls((1, D_MODEL)), ls((1, D_MODEL)),
            _whole_spec((D_MODEL, D_IN), **once),
            ls((D_POOL, D_POOL)), ls((1, D_POOL)),
            pl.BlockSpec((1, D_POOL), lambda b, t: (0, 0)),
            ls((SCONV_K, D_SCONV)),
            ls((CCONV_K, D_CCONV)), ls((1, D_CCONV)), ls((1, D_CCONV)), ls((1, D_CCONV)),
            _whole_spec((D_MODEL, D_MODEL), **once),
        ] + cast_in,
        out_specs=[tile, tail(POOL_PAST, D_POOL), tail(SCONV_K - 1, D_SCONV),
                   tail(CCONV_K - 1, D_CCONV)] + cast_out,
        out_shape=[
            jax.ShapeDtypeStruct((BATCH, SEQ, D_MODEL), F32),
            jax.ShapeDtypeStruct(tail_shape(POOL_PAST, D_POOL), F32),
            jax.ShapeDtypeStruct(tail_shape(SCONV_K - 1, D_SCONV), F32),
            jax.ShapeDtypeStruct(tail_shape(CCONV_K - 1, D_CCONV), F32),
        ] + cast_shape,
        scratch_shapes=[
            pltpu.VMEM((SUBLANES * POOL_PAST, D_POOL), F32),
            pltpu.VMEM((SUBLANES * (SCONV_K - 1), D_SCONV), F32),
            pltpu.VMEM((SUBLANES * (CCONV_K - 1), D_CCONV), F32),
        ],
        compiler_params=pltpu.CompilerParams(
            dimension_semantics=("arbitrary", "arbitrary"),
            vmem_limit_bytes=VMEM_LIMIT_BYTES),
        name="mix_prompt",
    )(x, mod, p["g_mix_pre"], p["g_mix_post"], w["w_in"], p["wpool"], p["pool_scale"],
      p["wl"], p["sconv_w"], p["cconv_w"], p["cconv_b"], p["cln_g"], p["cln_b"], w["w_out"],
      *(next_f32 or ()))


def _ffn_prompt_kernel(x_ref, mod_ref, gpre_ref, gpost_ref, wu_ref, fw_ref, wd_ref,
                       xo_ref, tup_ref, tail_up, act_buf):
    b = pl.program_id(0)
    t = pl.program_id(1)
    n_back = FFN_K - 1

    @pl.when(t == 0)
    def _():
        tail_up[...] = jnp.zeros(tail_up.shape, F32)

    x = x_ref[0]
    sh2 = mod_ref[pl.ds(b, 1), 3 * D_MODEL:4 * D_MODEL]
    sc2 = mod_ref[pl.ds(b, 1), 4 * D_MODEL:5 * D_MODEL]
    gt2 = mod_ref[pl.ds(b, 1), 5 * D_MODEL:6 * D_MODEL]
    hb = (_rms_scaled(x, gpre_ref[...] * (1.0 + sc2)) + sh2).astype(BF16)

    def conv(lo):
        cols = slice(lo, lo + FF_CHUNK)
        cur = _dot(hb, wu_ref[:, cols])
        ext = jnp.concatenate([_halo_groups(tail_up[:, cols], cur, n_back), cur], axis=0)
        tail_up[:, cols] = cur[TM - SUBLANES * n_back:]
        out = fw_ref[FFN_K - 1:FFN_K, cols] * cur
        for k in range(n_back):
            out = out + fw_ref[k:k + 1, cols] * _back(ext, n_back, n_back - k)
        return out

    for j in range(N_FF_CHUNKS):
        lo = j * FF_CHUNK
        act_buf[:, lo:lo + FF_CHUNK] = (_silu(conv(lo)) * conv(D_FF + lo)).astype(BF16)
    ff = _dot(act_buf[...], wd_ref[...])

    xo_ref[0] = x + _rms_scaled(ff, gt2 * gpost_ref[...])

    @pl.when(t == N_SEQ_TILES - 1)
    def _():
        tup_ref[0] = tail_up[...]


def _ffn_prompt_call(l, x, mod, p, w):
    tile = pl.BlockSpec((1, TM, D_MODEL), lambda b, t: (b, t, 0))
    ls = functools.partial(_layer_spec, l=l)
    once = dict(pipeline_mode=pl.Buffered(1))
    n_back = FFN_K - 1
    return pl.pallas_call(
        _ffn_prompt_kernel,
        grid=(BATCH, N_SEQ_TILES),
        in_specs=[
            tile,
            pl.BlockSpec((None, BATCH, 6 * D_MODEL), lambda b, t: (l, DEC_BATCH // BATCH, 0)),
            ls((1, D_MODEL)), ls((1, D_MODEL)),
            _whole_spec((D_MODEL, 2 * D_FF), **once),
            ls((FFN_K, 2 * D_FF)),
            _whole_spec((D_FF, D_MODEL), **once),
        ],
        out_specs=[tile, pl.BlockSpec((1, SUBLANES * n_back, 2 * D_FF), lambda b, t: (b, 0, 0))],
        out_shape=[
            jax.ShapeDtypeStruct((BATCH, SEQ, D_MODEL), F32),
            jax.ShapeDtypeStruct((BATCH, SUBLANES * n_back, 2 * D_FF), F32),
        ],
        scratch_shapes=[pltpu.VMEM((SUBLANES * n_back, 2 * D_FF), F32),
                        pltpu.VMEM((TM, D_FF), BF16)],
        compiler_params=pltpu.CompilerParams(
            dimension_semantics=("arbitrary", "arbitrary"),
            vmem_limit_bytes=VMEM_LIMIT_BYTES),
        name="ffn_prompt",
    )(x, mod, p["g_ffn_pre"], p["g_ffn_post"], w["w_up"], p["ffn_conv_w"], w["w_down"])


def _slab(a, t):
    return a[t * DEC_BATCH:(t + 1) * DEC_BATCH]


def _mix_dec_kernel(x_ref, mod_ref, gpre_ref, gpost_ref, win_ref, wpool_ref, pscale_ref,
                    wl_ref, sw_ref, cw_ref, cb_ref, lg_ref, lb_ref, wout_ref,
                    spool_ref, sz_ref, sglu_ref,
                    xo_ref, npool_ref, nz_ref, nglu_ref):
    x = x_ref[...]
    rep = lambda a: jnp.concatenate([a] * DEC_SEQ, axis=0)
    sh1 = rep(mod_ref[:, 0:D_MODEL])
    sc1 = rep(mod_ref[:, D_MODEL:2 * D_MODEL])
    gt1 = rep(mod_ref[:, 2 * D_MODEL:3 * D_MODEL])
    h = _rms_scaled(x, gpre_ref[...] * (1.0 + sc1)) + sh1
    u, z_b, bg, glu = _mixer_heads(_dot(h.astype(BF16), win_ref[...]))

    def ext(state_ref, cur, n_past, j):
        return state_ref[j] if j < n_past else _slab(cur, j - n_past)

    for new_ref, state_ref, cur in ((npool_ref, spool_ref, u), (nz_ref, sz_ref, z_b),
                                    (nglu_ref, sglu_ref, glu)):
        n_past = state_ref.shape[0]
        for k in range(n_past):
            new_ref[k] = ext(state_ref, cur, n_past, k + DEC_SEQ)

    wins, convs_b, convs_c, cnts = [], [], [], []
    for t in range(DEC_SEQ):
        wins.append(_pool_windows(
            lambda i, cols: ext(spool_ref, u, POOL_PAST, POOL_PAST + t - i)[:, cols]))
        cnts.append(jnp.broadcast_to(jnp.minimum(F32(PAST_LEN + t + 1), wl_ref[...]),
                                     (DEC_BATCH, D_POOL)))
        acc = None
        for k in range(SCONV_K):
            term = sw_ref[k:k + 1, :] * ext(sz_ref, z_b, SCONV_K - 1, t + k)
            acc = term if acc is None else acc + term
        convs_b.append(acc)
        acc = None
        for k in range(CCONV_K):
            term = cw_ref[k:k + 1, :] * ext(sglu_ref, glu, CCONV_K - 1, t + k)
            acc = term if acc is None else acc + term
        convs_c.append(acc)
    cat = lambda xs: jnp.concatenate(xs, axis=0)
    xo_ref[...] = _mixer_out(x, gt1, gpost_ref, cat(wins), cat(cnts), u, cat(convs_b), bg,
                             cat(convs_c), wpool_ref, pscale_ref, cb_ref, lg_ref, lb_ref,
                             wout_ref)


def _full_spec(shape):
    return pl.BlockSpec(shape, lambda i: (0,) * len(shape))


def _mix_dec_call(l, x, mod, spool, sz, sglu, p, w):
    ls = functools.partial(_layer_spec, l=l)
    states = (spool, sz, sglu)
    return pl.pallas_call(
        _mix_dec_kernel,
        grid=(1,),
        in_specs=[
            _full_spec((DEC_ROWS, D_MODEL)), ls((DEC_BATCH, 6 * D_MODEL)),
            ls((1, D_MODEL)), ls((1, D_MODEL)),
            _whole_spec((D_MODEL, D_IN)),
            ls((D_POOL, D_POOL)), ls((1, D_POOL)),
            _full_spec((1, D_POOL)),
            ls((SCONV_K, D_SCONV)),
            ls((CCONV_K, D_CCONV)), ls((1, D_CCONV)), ls((1, D_CCONV)), ls((1, D_CCONV)),
            _whole_spec((D_MODEL, D_MODEL)),
        ] + [ls(s.shape[1:]) for s in states],
        out_specs=[_full_spec((DEC_ROWS, D_MODEL))] + [_full_spec(s.shape[1:]) for s in states],
        out_shape=[jax.ShapeDtypeStruct((DEC_ROWS, D_MODEL), F32)]
        + [jax.ShapeDtypeStruct(s.shape[1:], F32) for s in states],
        compiler_params=pltpu.CompilerParams(
            dimension_semantics=("arbitrary",), vmem_limit_bytes=VMEM_LIMIT_BYTES),
        name="mix_dec",
    )(x, mod, p["g_mix_pre"], p["g_mix_post"], w["w_in"], p["wpool"], p["pool_scale"],
      p["wl"], p["sconv_w"], p["cconv_w"], p["cconv_b"], p["cln_g"], p["cln_b"],
      w["w_out"], *states)


def _ffn_dec_kernel(x_ref, mod_ref, gpre_ref, gpost_ref, wu_ref, fw_ref, wd_ref, sup_ref,
                    xo_ref, up_ref, act_buf):
    x = x_ref[...]
    rep = lambda a: jnp.concatenate([a] * DEC_SEQ, axis=0)
    sh2 = rep(mod_ref[:, 3 * D_MODEL:4 * D_MODEL])
    sc2 = rep(mod_ref[:, 4 * D_MODEL:5 * D_MODEL])
    gt2 = rep(mod_ref[:, 5 * D_MODEL:6 * D_MODEL])
    hb = (_rms_scaled(x, gpre_ref[...] * (1.0 + sc2)) + sh2).astype(BF16)
    n_past = FFN_K - 1

    def conv(lo):
        cols = slice(lo, lo + FF_CHUNK)
        cur = _dot(hb, wu_ref[:, cols])
        for k in range(n_past):
            up_ref[k, :, cols] = _slab(cur, DEC_SEQ - n_past + k)
        outs = []
        for t in range(DEC_SEQ):
            acc = None
            for k in range(FFN_K):
                j = t + k
                row = sup_ref[j, :, cols] if j < n_past else _slab(cur, j - n_past)
                term = fw_ref[k:k + 1, cols] * row
                acc = term if acc is None else acc + term
            outs.append(acc)
        return jnp.concatenate(outs, axis=0)

    for j in range(N_FF_CHUNKS):
        lo = j * FF_CHUNK
        act_buf[:, lo:lo + FF_CHUNK] = (_silu(conv(lo)) * conv(D_FF + lo)).astype(BF16)
    ff = _dot(act_buf[...], wd_ref[...])
    xo_ref[...] = x + _rms_scaled(ff, gt2 * gpost_ref[...])


def _ffn_dec_call(l, x, mod, sup, p, w):
    ls = functools.partial(_layer_spec, l=l)
    return pl.pallas_call(
        _ffn_dec_kernel,
        grid=(1,),
        in_specs=[
            _full_spec((DEC_ROWS, D_MODEL)), ls((DEC_BATCH, 6 * D_MODEL)),
            ls((1, D_MODEL)), ls((1, D_MODEL)),
            _whole_spec((D_MODEL, 2 * D_FF)),
            ls((FFN_K, 2 * D_FF)),
            _whole_spec((D_FF, D_MODEL)),
            ls(sup.shape[1:]),
        ],
        out_specs=[_full_spec((DEC_ROWS, D_MODEL)), _full_spec(sup.shape[1:])],
        out_shape=[
            jax.ShapeDtypeStruct((DEC_ROWS, D_MODEL), F32),
            jax.ShapeDtypeStruct(sup.shape[1:], F32),
        ],
        scratch_shapes=[pltpu.VMEM((DEC_ROWS, D_FF), BF16)],
        compiler_params=pltpu.CompilerParams(
            dimension_semantics=("arbitrary",), vmem_limit_bytes=VMEM_LIMIT_BYTES),
        name="ffn_dec",
    )(x, mod, p["g_ffn_pre"], p["g_ffn_post"], w["w_up"], p["ffn_conv_w"], w["w_down"], sup)


def _block_diag(pool_w):
    g = len(POOL_WINDOWS)
    eye = jnp.eye(g, dtype=bool)[None, :, None, :, None]
    full = jnp.where(eye, pool_w[:, :, :, None, :], 0.0)
    return full.reshape(DEPTH, D_POOL, D_POOL)


def _interleave_glu_columns(w_in):
    parts = [w_in[..., :O_AC]]
    for q in range(N_GLU_TILES):
        parts.append(w_in[..., O_AC + q * LANES:O_AC + (q + 1) * LANES])
        parts.append(w_in[..., O_BC + q * LANES:O_BC + (q + 1) * LANES])
    return jnp.concatenate(parts, axis=-1)


def _tile_permute(x, inverse=False):
    shape = (BATCH, N_SEQ_TILES) + ((SEG, SUBLANES) if inverse else (SUBLANES, SEG)) + (D_MODEL,)
    return x.reshape(shape).swapaxes(2, 3).reshape(BATCH, SEQ, D_MODEL)


def kernel(x_prompt, x_sample, c_prompt, c_sample, state_pool, state_sconv, state_cconv, state_ffn,
           w_ada, b_ada, g_mix_pre, g_mix_post, g_ffn_pre, g_ffn_post, w_in, pool_w, pool_scale,
           sconv_w, cconv_w, cconv_b, cln_g, cln_b, w_out, w_up, ffn_conv_w, w_down):
    mod = _ada_call(jnp.concatenate([c_sample, c_prompt], axis=0), w_ada, b_ada)
    row = lambda a: a[:, None, :]
    p = dict(
        g_mix_pre=row(g_mix_pre), g_mix_post=row(g_mix_post),
        g_ffn_pre=row(g_ffn_pre), g_ffn_post=row(g_ffn_post),
        wpool=_block_diag(pool_w).astype(BF16),
        pool_scale=row(pool_scale),
        wl=jnp.repeat(jnp.asarray(POOL_WINDOWS, F32), POOL_GC)[None, :],
        sconv_w=sconv_w, cconv_w=cconv_w, cconv_b=row(cconv_b),
        cln_g=row(cln_g), cln_b=row(cln_b),
        ffn_conv_w=ffn_conv_w,
    )
    names = ("w_in", "w_out", "w_up", "w_down")
    stacked = (w_in, w_out, w_up, w_down)
    w = dict(zip(names, (_interleave_glu_columns(w_in[0]).astype(BF16), w_out[0].astype(BF16),
                         w_up[0].astype(BF16), w_down[0].astype(BF16))))
    spool, sz, sglu, sup = (jnp.swapaxes(s, 1, 2)
                            for s in (state_pool, state_sconv, state_cconv, state_ffn))

    xp = _tile_permute(x_prompt)
    xs = jnp.swapaxes(x_sample, 0, 1).reshape(DEC_ROWS, D_MODEL)
    tails = [[] for _ in range(4)]
    news = [[] for _ in range(4)]
    for l in range(DEPTH):
        xp, tpool, tz, tglu, *w_next = _mix_prompt_call(
            l, xp, mod, p, w, stacked if l + 1 < DEPTH else None)
        xp, tup = _ffn_prompt_call(l, xp, mod, p, w)
        for acc, a in zip(tails, (tpool, tz, tglu, tup)):
            acc.append(a[:, SUBLANES - 1::SUBLANES])
        xs, npool, nz, nglu = _mix_dec_call(l, xs, mod, spool, sz, sglu, p, w)
        xs, nup = _ffn_dec_call(l, xs, mod, sup, p, w)
        w = dict(zip(names, w_next))
        for acc, a in zip(news, (npool, nz, nglu, nup)):
            acc.append(a)

    prompt_state = [jnp.stack(a) for a in tails]
    dec_state = [jnp.swapaxes(jnp.stack(a), 1, 2) for a in news]
    ys = jnp.swapaxes(xs.reshape(DEC_SEQ, DEC_BATCH, D_MODEL), 0, 1)
    return (_tile_permute(xp, inverse=True), ys) + tuple(prompt_state) + tuple(dec_state)
```

```python
import functools

import jax
import jax.numpy as jnp
from jax import lax
from jax.experimental import pallas as pl
from jax.experimental.pallas import tpu as pltpu

D_MODEL = 1024
BATCH = 8
SEQ = 2048
DEPTH = 4
DEC_BATCH = 128
DEC_SEQ = 4
PAST_LEN = 16384

D_POOL = 256
POOL_WINDOWS = (2, 4, 8, 16)
POOL_GC = 64
POOL_PAST = 15
D_SCONV = 384
D_CCONV = 384
SCONV_K = 3
CCONV_K = 31
FFN_K = 3
D_FF = 2816
D_IN = D_POOL + 3 * D_SCONV + 2 * D_CCONV
EPS = 1e-6

LANES = 128
SUBLANES = 8

O_HB = D_POOL
O_BG = O_HB + D_SCONV
O_CG = O_BG + D_SCONV
O_AC = O_CG + D_SCONV
O_BC = O_AC + D_CCONV
O_GLU = O_AC
N_GLU_TILES = D_CCONV // LANES

FF_CHUNK = 256
N_FF_CHUNKS = D_FF // FF_CHUNK
TM = 1024
N_SEQ_TILES = SEQ // TM
SUB = 512
N_SUB = TM // SUB
SEG = SUB // SUBLANES
DEC_ROWS = DEC_BATCH * DEC_SEQ
N_ADA = BATCH + DEC_BATCH
ADA_TN = 1024
VMEM_LIMIT_BYTES = 56 * 1024 * 1024

F32 = jnp.float32
BF16 = jnp.bfloat16

assert 2 * POOL_GC == LANES and len(POOL_WINDOWS) == 4
assert SEG >= CCONV_K - 1


def _rms_scaled(x, scale):
    return x * lax.rsqrt(jnp.mean(x * x, axis=-1, keepdims=True) + EPS) * scale


def _layer_norm(x, g, b):
    mu = jnp.mean(x, axis=-1, keepdims=True)
    xc = x - mu
    var = jnp.mean(xc * xc, axis=-1, keepdims=True)
    return xc * lax.rsqrt(var + EPS) * g + b


def _silu(x):
    return x * jax.nn.sigmoid(x)


def _dot(a, b):
    return jnp.dot(a, b, preferred_element_type=F32)


def _layer_spec(shape, l, **kw):
    return pl.BlockSpec((None,) + shape, lambda *_: (l,) + (0,) * len(shape), **kw)


def _whole_spec(shape, **kw):
    return pl.BlockSpec(shape, lambda *_: (0,) * len(shape), **kw)


def _ada_kernel(c_ref, w_ref, b_ref, o_ref):
    s = _silu(c_ref[...]).astype(BF16)
    o_ref[...] = _dot(s, w_ref[...].astype(BF16)) + b_ref[...]


def _ada_call(c_all, w_ada, b_ada):
    return pl.pallas_call(
        _ada_kernel,
        grid=(DEPTH, 6 * D_MODEL // ADA_TN),
        in_specs=[
            pl.BlockSpec((N_ADA, D_MODEL), lambda l, n: (0, 0)),
            pl.BlockSpec((None, D_MODEL, ADA_TN), lambda l, n: (l, 0, n)),
            pl.BlockSpec((None, 1, ADA_TN), lambda l, n: (l, 0, n)),
        ],
        out_specs=pl.BlockSpec((None, N_ADA, ADA_TN), lambda l, n: (l, 0, n)),
        out_shape=jax.ShapeDtypeStruct((DEPTH, N_ADA, 6 * D_MODEL), F32),
        compiler_params=pltpu.CompilerParams(
            dimension_semantics=("arbitrary", "arbitrary"),
            vmem_limit_bytes=VMEM_LIMIT_BYTES),
        name="ada",
    )(c_all, w_ada, b_ada.reshape(DEPTH, 1, 6 * D_MODEL))


def _mixer_heads(proj):
    u = proj[:, 0:O_HB]
    z_b = proj[:, O_CG:O_AC] * proj[:, O_HB:O_BG]
    bg = proj[:, O_BG:O_CG]
    glu = []
    for q in range(N_GLU_TILES):
        a = proj[:, O_GLU + 2 * q * LANES:O_GLU + (2 * q + 1) * LANES]
        b = proj[:, O_GLU + (2 * q + 1) * LANES:O_GLU + (2 * q + 2) * LANES]
        glu.append(a * jax.nn.sigmoid(b))
    return u, z_b, bg, jnp.concatenate(glu, axis=-1)


def _pool_windows(tap):
    halves = []
    for h in range(2):
        w_small, w_big = POOL_WINDOWS[2 * h], POOL_WINDOWS[2 * h + 1]
        cols = slice(h * LANES, (h + 1) * LANES)
        s = tap(0, cols)
        for i in range(1, w_small):
            s = s + tap(i, cols)
        small = s
        for i in range(w_small, w_big):
            s = s + tap(i, cols)
        lane = lax.broadcasted_iota(jnp.int32, s.shape, 1)
        halves.append(jnp.where(lane < POOL_GC, small, s))
    return jnp.concatenate(halves, axis=-1)


def _mixer_out(x, gt1, gpost_ref, win, cnt, u, conv_b, bg, conv_c,
               wpool_ref, pscale_ref, cb_ref, lg_ref, lb_ref, wout_ref):
    pooled = (win / cnt - u).astype(BF16)
    y_a = (_dot(pooled, wpool_ref[...]) * pscale_ref[...]).astype(BF16)
    y_b = (bg * conv_b).astype(BF16)
    y_c = _silu(_layer_norm(conv_c + cb_ref[...], lg_ref[...], lb_ref[...])).astype(BF16)
    mix = _dot(jnp.concatenate([y_a, y_b, y_c], axis=-1), wout_ref[...])
    return x + _rms_scaled(mix, gt1 * gpost_ref[...])


def _halo_groups(prev_tail, cur, n_back):
    tail = cur[SUB - SUBLANES * n_back:]
    sub = lax.broadcasted_iota(jnp.int32, tail.shape, 0) & (SUBLANES - 1)
    mixed = jnp.where(sub == SUBLANES - 1, prev_tail, tail)
    groups = [pltpu.roll(mixed[SUBLANES * i:SUBLANES * (i + 1)], 1, axis=0) for i in range(n_back)]
    return jnp.concatenate(groups, axis=0)


def _last_positions(out_ref, tail_ref):
    for k in range(out_ref.shape[1]):
        row = SUBLANES * k + SUBLANES - 1
        out_ref[0, k:k + 1, :] = tail_ref[row:row + 1, :]


def _back(ext, n_back, j):
    lo = SUBLANES * (n_back - j)
    return ext[lo:lo + SUB]


def _mix_prompt_kernel(x_ref, mod_ref, gpre_ref, gpost_ref, win_ref, wpool_ref, pscale_ref,
                       wl_ref, sw_ref, cw_ref, cb_ref, lg_ref, lb_ref, wout_ref, *rest,
                       cast_next):
    b = pl.program_id(0)
    t = pl.program_id(1)
    if cast_next:
        next_f32, rest = rest[:4], rest[4:]
        xo_ref, tpool_ref, tz_ref, tglu_ref = rest[:4]
        next_bf16, rest = rest[4:8], rest[8:]
        next_bf16[0][...] = _interleave_glu_columns(next_f32[0][...]).astype(BF16)
        for src, dst in zip(next_f32[1:], next_bf16[1:]):
            dst[...] = src[...].astype(BF16)
    else:
        xo_ref, tpool_ref, tz_ref, tglu_ref = rest[:4]
        rest = rest[4:]
    tail_u, tail_z, tail_glu = rest

    @pl.when(t == 0)
    def _():
        tail_u[...] = jnp.zeros(tail_u.shape, F32)
        tail_z[...] = jnp.zeros(tail_z.shape, F32)
        tail_glu[...] = jnp.zeros(tail_glu.shape, F32)

    sh1 = mod_ref[pl.ds(b, 1), 0:D_MODEL]
    sc1 = mod_ref[pl.ds(b, 1), D_MODEL:2 * D_MODEL]
    gt1 = mod_ref[pl.ds(b, 1), 2 * D_MODEL:3 * D_MODEL]
    pre_scale = gpre_ref[...] * (1.0 + sc1)

    xs, heads = [], []
    for part in range(N_SUB):
        x = x_ref[0, part * SUB:(part + 1) * SUB, :]
        h = _rms_scaled(x, pre_scale) + sh1
        xs.append(x)
        heads.append(_mixer_heads(_dot(h.astype(BF16), win_ref[...])))

    for part in range(N_SUB):
        rows = slice(part * SUB, (part + 1) * SUB)
        x = xs[part]
        u, z_b, bg, glu = heads[part]

        eu = jnp.concatenate([_halo_groups(tail_u[...], u, POOL_PAST), u], axis=0)
        ez = jnp.concatenate([_halo_groups(tail_z[...], z_b, SCONV_K - 1), z_b], axis=0)
        eg = jnp.concatenate([_halo_groups(tail_glu[...], glu, CCONV_K - 1), glu], axis=0)
        tail_u[...] = u[SUB - SUBLANES * POOL_PAST:]
        tail_z[...] = z_b[SUB - SUBLANES * (SCONV_K - 1):]
        tail_glu[...] = glu[SUB - SUBLANES * (CCONV_K - 1):]

        win = _pool_windows(lambda i, cols: _back(eu, POOL_PAST, i)[:, cols])
        row = lax.broadcasted_iota(jnp.int32, (SUB, D_POOL), 0)
        pos1 = (t * N_SUB + part) * SUB + 1 + (row & (SUBLANES - 1)) * SEG + (row >> 3)
        cnt = jnp.minimum(pos1.astype(F32), wl_ref[...])

        conv_b = sw_ref[SCONV_K - 1:SCONV_K, :] * z_b
        for k in range(SCONV_K - 1):
            conv_b = conv_b + sw_ref[k:k + 1, :] * _back(ez, SCONV_K - 1, SCONV_K - 1 - k)
        conv_c = cw_ref[CCONV_K - 1:CCONV_K, :] * glu
        for k in range(CCONV_K - 1):
            conv_c = conv_c + cw_ref[k:k + 1, :] * _back(eg, CCONV_K - 1, CCONV_K - 1 - k)

        xo_ref[0, rows, :] = _mixer_out(x, gt1, gpost_ref, win, cnt, u, conv_b, bg, conv_c,
                                        wpool_ref, pscale_ref, cb_ref, lg_ref, lb_ref, wout_ref)

    @pl.when(t == N_SEQ_TILES - 1)
    def _():
        _last_positions(tpool_ref, tail_u)
        _last_positions(tz_ref, tail_z)
        _last_positions(tglu_ref, tail_glu)


def _mix_prompt_call(l, x, mod, p, w, next_f32):
    tile = pl.BlockSpec((1, TM, D_MODEL), lambda b, t: (b, t, 0))
    tail = lambda n_back, ch: pl.BlockSpec((1, n_back, ch), lambda b, t: (b, 0, 0))
    tail_shape = lambda n_back, ch: (BATCH, n_back, ch)
    once = dict(pipeline_mode=pl.Buffered(1))
    ls = functools.partial(_layer_spec, l=l)
    n_steps = BATCH * N_SEQ_TILES
    cast_in, cast_out, cast_shape = [], [], []
    for a in next_f32 or ():
        rows, cols = a.shape[1] // n_steps, a.shape[2]
        cast_in.append(pl.BlockSpec((None, rows, cols),
                                    lambda b, t: (l + 1, b * N_SEQ_TILES + t, 0)))
        cast_out.append(pl.BlockSpec((rows, cols), lambda b, t: (b * N_SEQ_TILES + t, 0)))
        cast_shape.append(jax.ShapeDtypeStruct(a.shape[1:], BF16))
    return pl.pallas_call(
        functools.partial(_mix_prompt_kernel, cast_next=bool(next_f32)),
        grid=(BATCH, N_SEQ_TILES),
        in_specs=[
            tile,
            pl.BlockSpec((None, BATCH, 6 * D_MODEL), lambda b, t: (l, DEC_BATCH // BATCH, 0)),
            ls((1, D_MODEL)), ls((1, D_MODEL)),
            _whole_spec((D_MODEL, D_IN), **once),
            ls((D_POOL, D_POOL)), ls((1, D_POOL)),
            pl.BlockSpec((1, D_POOL), lambda b, t: (0, 0)),
            ls((SCONV_K, D_SCONV)),
            ls((CCONV_K, D_CCONV)), ls((1, D_CCONV)), ls((1, D_CCONV)), ls((1, D_CCONV)),
            _whole_spec((D_MODEL, D_MODEL), **once),
        ] + cast_in,
        out_specs=[tile, tail(POOL_PAST, D_POOL), tail(SCONV_K - 1, D_SCONV),
                   tail(CCONV_K - 1, D_CCONV)] + cast_out,
        out_shape=[
            jax.ShapeDtypeStruct((BATCH, SEQ, D_MODEL), F32),
            jax.ShapeDtypeStruct(tail_shape(POOL_PAST, D_POOL), F32),
            jax.ShapeDtypeStruct(tail_shape(SCONV_K - 1, D_SCONV), F32),
            jax.ShapeDtypeStruct(tail_shape(CCONV_K - 1, D_CCONV), F32),
        ] + cast_shape,
        scratch_shapes=[
            pltpu.VMEM((SUBLANES * POOL_PAST, D_POOL), F32),
            pltpu.VMEM((SUBLANES * (SCONV_K - 1), D_SCONV), F32),
            pltpu.VMEM((SUBLANES * (CCONV_K - 1), D_CCONV), F32),
        ],
        compiler_params=pltpu.CompilerParams(
            dimension_semantics=("arbitrary", "arbitrary"),
            vmem_limit_bytes=VMEM_LIMIT_BYTES),
        name="mix_prompt",
    )(x, mod, p["g_mix_pre"], p["g_mix_post"], w["w_in"], p["wpool"], p["pool_scale"],
      p["wl"], p["sconv_w"], p["cconv_w"], p["cconv_b"], p["cln_g"], p["cln_b"], w["w_out"],
      *(next_f32 or ()))


def _ffn_prompt_kernel(x_ref, mod_ref, gpre_ref, gpost_ref, wu_ref, fw_ref, wd_ref,
                       xo_ref, tup_ref, tail_up, act_buf):
    b = pl.program_id(0)
    t = pl.program_id(1)
    n_back = FFN_K - 1

    @pl.when(t == 0)
    def _():
        tail_up[...] = jnp.zeros(tail_up.shape, F32)

    x = x_ref[0]
    sh2 = mod_ref[pl.ds(b, 1), 3 * D_MODEL:4 * D_MODEL]
    sc2 = mod_ref[pl.ds(b, 1), 4 * D_MODEL:5 * D_MODEL]
    gt2 = mod_ref[pl.ds(b, 1), 5 * D_MODEL:6 * D_MODEL]
    hb = (_rms_scaled(x, gpre_ref[...] * (1.0 + sc2)) + sh2).astype(BF16)

    def conv(lo):
        cols = slice(lo, lo + FF_CHUNK)
        up = _dot(hb, wu_ref[:, cols])
        prev = tail_up[:, cols]
        outs = []
        for part in range(N_SUB):
            cur = up[part * SUB:(part + 1) * SUB]
            ext = jnp.concatenate([_halo_groups(prev, cur, n_back), cur], axis=0)
            prev = cur[SUB - SUBLANES * n_back:]
            out = fw_ref[FFN_K - 1:FFN_K, cols] * cur
            for k in range(n_back):
                out = out + fw_ref[k:k + 1, cols] * _back(ext, n_back, n_back - k)
            outs.append(out)
        tail_up[:, cols] = prev
        return jnp.concatenate(outs, axis=0)

    for j in range(N_FF_CHUNKS):
        lo = j * FF_CHUNK
        act_buf[:, lo:lo + FF_CHUNK] = (_silu(conv(lo)) * conv(D_FF + lo)).astype(BF16)
    ff = _dot(act_buf[...], wd_ref[...])

    xo_ref[0] = x + _rms_scaled(ff, gt2 * gpost_ref[...])

    @pl.when(t == N_SEQ_TILES - 1)
    def _():
        _last_positions(tup_ref, tail_up)


def _ffn_prompt_call(l, x, mod, p, w):
    tile = pl.BlockSpec((1, TM, D_MODEL), lambda b, t: (b, t, 0))
    ls = functools.partial(_layer_spec, l=l)
    once = dict(pipeline_mode=pl.Buffered(1))
    n_back = FFN_K - 1
    return pl.pallas_call(
        _ffn_prompt_kernel,
        grid=(BATCH, N_SEQ_TILES),
        in_specs=[
            tile,
            pl.BlockSpec((None, BATCH, 6 * D_MODEL), lambda b, t: (l, DEC_BATCH // BATCH, 0)),
            ls((1, D_MODEL)), ls((1, D_MODEL)),
            _whole_spec((D_MODEL, 2 * D_FF), **once),
            ls((FFN_K, 2 * D_FF)),
            _whole_spec((D_FF, D_MODEL), **once),
        ],
        out_specs=[tile, pl.BlockSpec((1, n_back, 2 * D_FF), lambda b, t: (b, 0, 0))],
        out_shape=[
            jax.ShapeDtypeStruct((BATCH, SEQ, D_MODEL), F32),
            jax.ShapeDtypeStruct((BATCH, n_back, 2 * D_FF), F32),
        ],
        scratch_shapes=[pltpu.VMEM((SUBLANES * n_back, 2 * D_FF), F32),
                        pltpu.VMEM((TM, D_FF), BF16)],
        compiler_params=pltpu.CompilerParams(
            dimension_semantics=("arbitrary", "arbitrary"),
            vmem_limit_bytes=VMEM_LIMIT_BYTES),
        name="ffn_prompt",
    )(x, mod, p["g_ffn_pre"], p["g_ffn_post"], w["w_up"], p["ffn_conv_w"], w["w_down"])


def _slab(a, t):
    return a[t * DEC_BATCH:(t + 1) * DEC_BATCH]


def _mix_dec_kernel(x_ref, mod_ref, gpre_ref, gpost_ref, win_ref, wpool_ref, pscale_ref,
                    wl_ref, sw_ref, cw_ref, cb_ref, lg_ref, lb_ref, wout_ref,
                    spool_ref, sz_ref, sglu_ref,
                    xo_ref, npool_ref, nz_ref, nglu_ref):
    x = x_ref[...]
    rep = lambda a: jnp.concatenate([a] * DEC_SEQ, axis=0)
    sh1 = rep(mod_ref[:, 0:D_MODEL])
    sc1 = rep(mod_ref[:, D_MODEL:2 * D_MODEL])
    gt1 = rep(mod_ref[:, 2 * D_MODEL:3 * D_MODEL])
    h = _rms_scaled(x, gpre_ref[...] * (1.0 + sc1)) + sh1
    u, z_b, bg, glu = _mixer_heads(_dot(h.astype(BF16), win_ref[...]))

    def ext(state_ref, cur, n_past, j):
        return state_ref[j] if j < n_past else _slab(cur, j - n_past)

    for new_ref, state_ref, cur in ((npool_ref, spool_ref, u), (nz_ref, sz_ref, z_b),
                                    (nglu_ref, sglu_ref, glu)):
        n_past = state_ref.shape[0]
        for k in range(n_past):
            new_ref[k] = ext(state_ref, cur, n_past, k + DEC_SEQ)

    wins, convs_b, convs_c, cnts = [], [], [], []
    for t in range(DEC_SEQ):
        wins.append(_pool_windows(
            lambda i, cols: ext(spool_ref, u, POOL_PAST, POOL_PAST + t - i)[:, cols]))
        cnts.append(jnp.broadcast_to(jnp.minimum(F32(PAST_LEN + t + 1), wl_ref[...]),
                                     (DEC_BATCH, D_POOL)))
        acc = None
        for k in range(SCONV_K):
            term = sw_ref[k:k + 1, :] * ext(sz_ref, z_b, SCONV_K - 1, t + k)
            acc = term if acc is None else acc + term
        convs_b.append(acc)
        acc = None
        for k in range(CCONV_K):
            term = cw_ref[k:k + 1, :] * ext(sglu_ref, glu, CCONV_K - 1, t + k)
            acc = term if acc is None else acc + term
        convs_c.append(acc)
    cat = lambda xs: jnp.concatenate(xs, axis=0)
    xo_ref[...] = _mixer_out(x, gt1, gpost_ref, cat(wins), cat(cnts), u, cat(convs_b), bg,
                             cat(convs_c), wpool_ref, pscale_ref, cb_ref, lg_ref, lb_ref,
                             wout_ref)


def _full_spec(shape):
    return pl.BlockSpec(shape, lambda i: (0,) * len(shape))


def _mix_dec_call(l, x, mod, spool, sz, sglu, p, w):
    ls = functools.partial(_layer_spec, l=l)
    states = (spool, sz, sglu)
    return pl.pallas_call(
        _mix_dec_kernel,
        grid=(1,),
        in_specs=[
            _full_spec((DEC_ROWS, D_MODEL)), ls((DEC_BATCH, 6 * D_MODEL)),
            ls((1, D_MODEL)), ls((1, D_MODEL)),
            _whole_spec((D_MODEL, D_IN)),
            ls((D_POOL, D_POOL)), ls((1, D_POOL)),
            _full_spec((1, D_POOL)),
            ls((SCONV_K, D_SCONV)),
            ls((CCONV_K, D_CCONV)), ls((1, D_CCONV)), ls((1, D_CCONV)), ls((1, D_CCONV)),
            _whole_spec((D_MODEL, D_MODEL)),
        ] + [ls(s.shape[1:]) for s in states],
        out_specs=[_full_spec((DEC_ROWS, D_MODEL))] + [_full_spec(s.shape[1:]) for s in states],
        out_shape=[jax.ShapeDtypeStruct((DEC_ROWS, D_MODEL), F32)]
        + [jax.ShapeDtypeStruct(s.shape[1:], F32) for s in states],
        compiler_params=pltpu.CompilerParams(
            dimension_semantics=("arbitrary",), vmem_limit_bytes=VMEM_LIMIT_BYTES),
        name="mix_dec",
    )(x, mod, p["g_mix_pre"], p["g_mix_post"], w["w_in"], p["wpool"], p["pool_scale"],
      p["wl"], p["sconv_w"], p["cconv_w"], p["cconv_b"], p["cln_g"], p["cln_b"],
      w["w_out"], *states)


def _ffn_dec_kernel(x_ref, mod_ref, gpre_ref, gpost_ref, wu_ref, fw_ref, wd_ref, sup_ref,
                    xo_ref, up_ref, act_buf):
    x = x_ref[...]
    rep = lambda a: jnp.concatenate([a] * DEC_SEQ, axis=0)
    sh2 = rep(mod_ref[:, 3 * D_MODEL:4 * D_MODEL])
    sc2 = rep(mod_ref[:, 4 * D_MODEL:5 * D_MODEL])
    gt2 = rep(mod_ref[:, 5 * D_MODEL:6 * D_MODEL])
    hb = (_rms_scaled(x, gpre_ref[...] * (1.0 + sc2)) + sh2).astype(BF16)
    n_past = FFN_K - 1

    def conv(lo):
        cols = slice(lo, lo + FF_CHUNK)
        cur = _dot(hb, wu_ref[:, cols])
        for k in range(n_past):
            up_ref[k, :, cols] = _slab(cur, DEC_SEQ - n_past + k)
        outs = []
        for t in range(DEC_SEQ):
            acc = None
            for k in range(FFN_K):
                j = t + k
                row = sup_ref[j, :, cols] if j < n_past else _slab(cur, j - n_past)
                term = fw_ref[k:k + 1, cols] * row
                acc = term if acc is None else acc + term
            outs.append(acc)
        return jnp.concatenate(outs, axis=0)

    for j in range(N_FF_CHUNKS):
        lo = j * FF_CHUNK
        act_buf[:, lo:lo + FF_CHUNK] = (_silu(conv(lo)) * conv(D_FF + lo)).astype(BF16)
    ff = _dot(act_buf[...], wd_ref[...])
    xo_ref[...] = x + _rms_scaled(ff, gt2 * gpost_ref[...])


def _ffn_dec_call(l, x, mod, sup, p, w):
    ls = functools.partial(_layer_spec, l=l)
    return pl.pallas_call(
        _ffn_dec_kernel,
        grid=(1,),
        in_specs=[
            _full_spec((DEC_ROWS, D_MODEL)), ls((DEC_BATCH, 6 * D_MODEL)),
            ls((1, D_MODEL)), ls((1, D_MODEL)),
            _whole_spec((D_MODEL, 2 * D_FF)),
            ls((FFN_K, 2 * D_FF)),
            _whole_spec((D_FF, D_MODEL)),
            ls(sup.shape[1:]),
        ],
        out_specs=[_full_spec((DEC_ROWS, D_MODEL)), _full_spec(sup.shape[1:])],
        out_shape=[
            jax.ShapeDtypeStruct((DEC_ROWS, D_MODEL), F32),
            jax.ShapeDtypeStruct(sup.shape[1:], F32),
        ],
        scratch_shapes=[pltpu.VMEM((DEC_ROWS, D_FF), BF16)],
        compiler_params=pltpu.CompilerParams(
            dimension_semantics=("arbitrary",), vmem_limit_bytes=VMEM_LIMIT_BYTES),
        name="ffn_dec",
    )(x, mod, p["g_ffn_pre"], p["g_ffn_post"], w["w_up"], p["ffn_conv_w"], w["w_down"], sup)


def _block_diag(pool_w):
    g = len(POOL_WINDOWS)
    eye = jnp.eye(g, dtype=bool)[None, :, None, :, None]
    full = jnp.where(eye, pool_w[:, :, :, None, :], 0.0)
    return full.reshape(DEPTH, D_POOL, D_POOL)


def _interleave_glu_columns(w_in):
    parts = [w_in[..., :O_AC]]
    for q in range(N_GLU_TILES):
        parts.append(w_in[..., O_AC + q * LANES:O_AC + (q + 1) * LANES])
        parts.append(w_in[..., O_BC + q * LANES:O_BC + (q + 1) * LANES])
    return jnp.concatenate(parts, axis=-1)


def _tile_permute(x, inverse=False):
    shape = (BATCH, SEQ // SUB) + ((SEG, SUBLANES) if inverse else (SUBLANES, SEG)) + (D_MODEL,)
    return x.reshape(shape).swapaxes(2, 3).reshape(BATCH, SEQ, D_MODEL)


def kernel(x_prompt, x_sample, c_prompt, c_sample, state_pool, state_sconv, state_cconv, state_ffn,
           w_ada, b_ada, g_mix_pre, g_mix_post, g_ffn_pre, g_ffn_post, w_in, pool_w, pool_scale,
           sconv_w, cconv_w, cconv_b, cln_g, cln_b, w_out, w_up, ffn_conv_w, w_down):
    mod = _ada_call(jnp.concatenate([c_sample, c_prompt], axis=0), w_ada, b_ada)
    row = lambda a: a[:, None, :]
    p = dict(
        g_mix_pre=row(g_mix_pre), g_mix_post=row(g_mix_post),
        g_ffn_pre=row(g_ffn_pre), g_ffn_post=row(g_ffn_post),
        wpool=_block_diag(pool_w).astype(BF16),
        pool_scale=row(pool_scale),
        wl=jnp.repeat(jnp.asarray(POOL_WINDOWS, F32), POOL_GC)[None, :],
        sconv_w=sconv_w, cconv_w=cconv_w, cconv_b=row(cconv_b),
        cln_g=row(cln_g), cln_b=row(cln_b),
        ffn_conv_w=ffn_conv_w,
    )
    names = ("w_in", "w_out", "w_up", "w_down")
    stacked = (w_in, w_out, w_up, w_down)
    w = dict(zip(names, (_interleave_glu_columns(w_in[0]).astype(BF16), w_out[0].astype(BF16),
                         w_up[0].astype(BF16), w_down[0].astype(BF16))))
    spool, sz, sglu, sup = (jnp.swapaxes(s, 1, 2)
                            for s in (state_pool, state_sconv, state_cconv, state_ffn))

    xp = _tile_permute(x_prompt)
    xs = jnp.swapaxes(x_sample, 0, 1).reshape(DEC_ROWS, D_MODEL)
    tails = [[] for _ in range(4)]
    news = [[] for _ in range(4)]
    for l in range(DEPTH):
        xp, tpool, tz, tglu, *w_next = _mix_prompt_call(
            l, xp, mod, p, w, stacked if l + 1 < DEPTH else None)
        xp, tup = _ffn_prompt_call(l, xp, mod, p, w)
        for acc, a in zip(tails, (tpool, tz, tglu, tup)):
            acc.append(a)
        xs, npool, nz, nglu = _mix_dec_call(l, xs, mod, spool, sz, sglu, p, w)
        xs, nup = _ffn_dec_call(l, xs, mod, sup, p, w)
        w = dict(zip(names, w_next))
        for acc, a in zip(news, (npool, nz, nglu, nup)):
            acc.append(a)

    prompt_state = [jnp.stack(a) for a in tails]
    dec_state = [jnp.swapaxes(jnp.stack(a), 1, 2) for a in news]
    ys = jnp.swapaxes(xs.reshape(DEC_SEQ, DEC_BATCH, D_MODEL), 0, 1)
    return (_tile_permute(xp, inverse=True), ys) + tuple(prompt_state) + tuple(dec_state)
```

```python
import functools

import jax
import jax.numpy as jnp
from jax import lax
from jax.experimental import pallas as pl
from jax.experimental.pallas import tpu as pltpu

D_MODEL = 1024
BATCH = 8
SEQ = 2048
DEPTH = 4
DEC_BATCH = 128
DEC_SEQ = 4
PAST_LEN = 16384

D_POOL = 256
POOL_WINDOWS = (2, 4, 8, 16)
POOL_GC = 64
POOL_PAST = 15
D_SCONV = 384
D_CCONV = 384
SCONV_K = 3
CCONV_K = 31
FFN_K = 3
D_FF = 2816
D_IN = D_POOL + 3 * D_SCONV + 2 * D_CCONV
EPS = 1e-6

LANES = 128
SUBLANES = 8

O_HB = D_POOL
O_BG = O_HB + D_SCONV
O_CG = O_BG + D_SCONV
O_AC = O_CG + D_SCONV
O_BC = O_AC + D_CCONV
O_GLU = O_AC
N_GLU_TILES = D_CCONV // LANES

FF_CHUNK = 256
N_FF_CHUNKS = D_FF // FF_CHUNK
TM = 1024
N_SEQ_TILES = SEQ // TM
TM_MIX = 1024
MIX_TILES = SEQ // TM_MIX
SUB = 512
N_SUB = TM // SUB
MIX_SUBS = TM_MIX // SUB
SEG = SUB // SUBLANES
DEC_ROWS = DEC_BATCH * DEC_SEQ
N_ADA = BATCH + DEC_BATCH
ADA_TN = 1024
VMEM_LIMIT_BYTES = 56 * 1024 * 1024

F32 = jnp.float32
BF16 = jnp.bfloat16

assert 2 * POOL_GC == LANES and len(POOL_WINDOWS) == 4
assert SEG >= CCONV_K - 1


def _rms_scaled(x, scale):
    return x * lax.rsqrt(jnp.mean(x * x, axis=-1, keepdims=True) + EPS) * scale


def _layer_norm(x, g, b):
    mu = jnp.mean(x, axis=-1, keepdims=True)
    xc = x - mu
    var = jnp.mean(xc * xc, axis=-1, keepdims=True)
    return xc * lax.rsqrt(var + EPS) * g + b


def _silu(x):
    return x * jax.nn.sigmoid(x)


def _dot(a, b):
    return jnp.dot(a, b, preferred_element_type=F32)


def _layer_spec(shape, l, **kw):
    return pl.BlockSpec((None,) + shape, lambda *_: (l,) + (0,) * len(shape), **kw)


def _whole_spec(shape, **kw):
    return pl.BlockSpec(shape, lambda *_: (0,) * len(shape), **kw)


def _ada_kernel(c_ref, w_ref, b_ref, o_ref):
    s = _silu(c_ref[...]).astype(BF16)
    o_ref[...] = _dot(s, w_ref[...].astype(BF16)) + b_ref[...]


def _ada_call(c_all, w_ada, b_ada):
    return pl.pallas_call(
        _ada_kernel,
        grid=(DEPTH, 6 * D_MODEL // ADA_TN),
        in_specs=[
            pl.BlockSpec((N_ADA, D_MODEL), lambda l, n: (0, 0)),
            pl.BlockSpec((None, D_MODEL, ADA_TN), lambda l, n: (l, 0, n)),
            pl.BlockSpec((None, 1, ADA_TN), lambda l, n: (l, 0, n)),
        ],
        out_specs=pl.BlockSpec((None, N_ADA, ADA_TN), lambda l, n: (l, 0, n)),
        out_shape=jax.ShapeDtypeStruct((DEPTH, N_ADA, 6 * D_MODEL), F32),
        compiler_params=pltpu.CompilerParams(
            dimension_semantics=("arbitrary", "arbitrary"),
            vmem_limit_bytes=VMEM_LIMIT_BYTES),
        name="ada",
    )(c_all, w_ada, b_ada.reshape(DEPTH, 1, 6 * D_MODEL))


def _mixer_heads(proj):
    u = proj[:, 0:O_HB]
    z_b = proj[:, O_CG:O_AC] * proj[:, O_HB:O_BG]
    bg = proj[:, O_BG:O_CG]
    glu = []
    for q in range(N_GLU_TILES):
        a = proj[:, O_GLU + 2 * q * LANES:O_GLU + (2 * q + 1) * LANES]
        b = proj[:, O_GLU + (2 * q + 1) * LANES:O_GLU + (2 * q + 2) * LANES]
        glu.append(a * jax.nn.sigmoid(b))
    return u, z_b, bg, jnp.concatenate(glu, axis=-1)


def _pool_windows(tap):
    halves = []
    for h in range(2):
        w_small, w_big = POOL_WINDOWS[2 * h], POOL_WINDOWS[2 * h + 1]
        cols = slice(h * LANES, (h + 1) * LANES)
        s = tap(0, cols)
        for i in range(1, w_small):
            s = s + tap(i, cols)
        small = s
        for i in range(w_small, w_big):
            s = s + tap(i, cols)
        lane = lax.broadcasted_iota(jnp.int32, s.shape, 1)
        halves.append(jnp.where(lane < POOL_GC, small, s))
    return jnp.concatenate(halves, axis=-1)


def _mixer_out(x, gt1, gpost_ref, win, cnt, u, conv_b, bg, conv_c,
               wpool_ref, pscale_ref, cb_ref, lg_ref, lb_ref, wout_ref):
    pooled = (win / cnt - u).astype(BF16)
    y_a = (_dot(pooled, wpool_ref[...]) * pscale_ref[...]).astype(BF16)
    y_b = (bg * conv_b).astype(BF16)
    y_c = _silu(_layer_norm(conv_c + cb_ref[...], lg_ref[...], lb_ref[...])).astype(BF16)
    mix = _dot(jnp.concatenate([y_a, y_b, y_c], axis=-1), wout_ref[...])
    return x + _rms_scaled(mix, gt1 * gpost_ref[...])


def _halo_groups(prev_tail, cur, n_back):
    tail = cur[SUB - SUBLANES * n_back:]
    sub = lax.broadcasted_iota(jnp.int32, tail.shape, 0) & (SUBLANES - 1)
    mixed = jnp.where(sub == SUBLANES - 1, prev_tail, tail)
    groups = [pltpu.roll(mixed[SUBLANES * i:SUBLANES * (i + 1)], 1, axis=0) for i in range(n_back)]
    return jnp.concatenate(groups, axis=0)


def _last_positions(out_ref, tail_ref):
    for k in range(out_ref.shape[1]):
        row = SUBLANES * k + SUBLANES - 1
        out_ref[0, k:k + 1, :] = tail_ref[row:row + 1, :]


def _back(ext, n_back, j):
    lo = SUBLANES * (n_back - j)
    return ext[lo:lo + SUB]


def _mix_prompt_kernel(x_ref, mod_ref, gpre_ref, gpost_ref, win_ref, wpool_ref, pscale_ref,
                       wl_ref, sw_ref, cw_ref, cb_ref, lg_ref, lb_ref, wout_ref, *rest,
                       cast_next):
    b = pl.program_id(0)
    t = pl.program_id(1)
    if cast_next:
        next_f32, rest = rest[:4], rest[4:]
        xo_ref, tpool_ref, tz_ref, tglu_ref = rest[:4]
        next_bf16, rest = rest[4:8], rest[8:]
        next_bf16[0][...] = _interleave_glu_columns(next_f32[0][...]).astype(BF16)
        for src, dst in zip(next_f32[1:], next_bf16[1:]):
            dst[...] = src[...].astype(BF16)
    else:
        xo_ref, tpool_ref, tz_ref, tglu_ref = rest[:4]
        rest = rest[4:]
    tail_u, tail_z, tail_glu = rest

    @pl.when(t == 0)
    def _():
        tail_u[...] = jnp.zeros(tail_u.shape, F32)
        tail_z[...] = jnp.zeros(tail_z.shape, F32)
        tail_glu[...] = jnp.zeros(tail_glu.shape, F32)

    sh1 = mod_ref[pl.ds(b, 1), 0:D_MODEL]
    sc1 = mod_ref[pl.ds(b, 1), D_MODEL:2 * D_MODEL]
    gt1 = mod_ref[pl.ds(b, 1), 2 * D_MODEL:3 * D_MODEL]
    pre_scale = gpre_ref[...] * (1.0 + sc1)

    xs, heads = [], []
    for part in range(MIX_SUBS):
        x = x_ref[0, part * SUB:(part + 1) * SUB, :]
        h = _rms_scaled(x, pre_scale) + sh1
        xs.append(x)
        heads.append(_mixer_heads(_dot(h.astype(BF16), win_ref[...])))

    for part in range(MIX_SUBS):
        rows = slice(part * SUB, (part + 1) * SUB)
        x = xs[part]
        u, z_b, bg, glu = heads[part]

        eu = jnp.concatenate([_halo_groups(tail_u[...], u, POOL_PAST), u], axis=0)
        ez = jnp.concatenate([_halo_groups(tail_z[...], z_b, SCONV_K - 1), z_b], axis=0)
        eg = jnp.concatenate([_halo_groups(tail_glu[...], glu, CCONV_K - 1), glu], axis=0)
        tail_u[...] = u[SUB - SUBLANES * POOL_PAST:]
        tail_z[...] = z_b[SUB - SUBLANES * (SCONV_K - 1):]
        tail_glu[...] = glu[SUB - SUBLANES * (CCONV_K - 1):]

        win = _pool_windows(lambda i, cols: _back(eu, POOL_PAST, i)[:, cols])
        row = lax.broadcasted_iota(jnp.int32, (SUB, D_POOL), 0)
        pos1 = (t * MIX_SUBS + part) * SUB + 1 + (row & (SUBLANES - 1)) * SEG + (row >> 3)
        cnt = jnp.minimum(pos1.astype(F32), wl_ref[...])

        conv_b = sw_ref[SCONV_K - 1:SCONV_K, :] * z_b
        for k in range(SCONV_K - 1):
            conv_b = conv_b + sw_ref[k:k + 1, :] * _back(ez, SCONV_K - 1, SCONV_K - 1 - k)
        conv_c = cw_ref[CCONV_K - 1:CCONV_K, :] * glu
        for k in range(CCONV_K - 1):
            conv_c = conv_c + cw_ref[k:k + 1, :] * _back(eg, CCONV_K - 1, CCONV_K - 1 - k)

        xo_ref[0, rows, :] = _mixer_out(x, gt1, gpost_ref, win, cnt, u, conv_b, bg, conv_c,
                                        wpool_ref, pscale_ref, cb_ref, lg_ref, lb_ref, wout_ref)

    @pl.when(t == MIX_TILES - 1)
    def _():
        _last_positions(tpool_ref, tail_u)
        _last_positions(tz_ref, tail_z)
        _last_positions(tglu_ref, tail_glu)


def _mix_prompt_call(l, x, mod, p, w, next_f32):
    tile = pl.BlockSpec((1, TM_MIX, D_MODEL), lambda b, t: (b, t, 0))
    tail = lambda n_back, ch: pl.BlockSpec((1, n_back, ch), lambda b, t: (b, 0, 0))
    tail_shape = lambda n_back, ch: (BATCH, n_back, ch)
    once = dict(pipeline_mode=pl.Buffered(1))
    ls = functools.partial(_layer_spec, l=l)
    n_steps = BATCH * MIX_TILES
    cast_in, cast_out, cast_shape = [], [], []
    for a in next_f32 or ():
        rows, cols = a.shape[1] // n_steps, a.shape[2]
        cast_in.append(pl.BlockSpec((None, rows, cols),
                                    lambda b, t: (l + 1, b * MIX_TILES + t, 0)))
        cast_out.append(pl.BlockSpec((rows, cols), lambda b, t: (b * MIX_TILES + t, 0)))
        cast_shape.append(jax.ShapeDtypeStruct(a.shape[1:], BF16))
    return pl.pallas_call(
        functools.partial(_mix_prompt_kernel, cast_next=bool(next_f32)),
        grid=(BATCH, MIX_TILES),
        in_specs=[
            tile,
            pl.BlockSpec((None, BATCH, 6 * D_MODEL), lambda b, t: (l, DEC_BATCH // BATCH, 0)),
            ls((1, D_MODEL)), ls((1, D_MODEL)),
            _whole_spec((D_MODEL, D_IN), **once),
            ls((D_POOL, D_POOL)), ls((1, D_POOL)),
            pl.BlockSpec((1, D_POOL), lambda b, t: (0, 0)),
            ls((SCONV_K, D_SCONV)),
            ls((CCONV_K, D_CCONV)), ls((1, D_CCONV)), ls((1, D_CCONV)), ls((1, D_CCONV)),
            _whole_spec((D_MODEL, D_MODEL), **once),
        ] + cast_in,
        out_specs=[tile, tail(POOL_PAST, D_POOL), tail(SCONV_K - 1, D_SCONV),
                   tail(CCONV_K - 1, D_CCONV)] + cast_out,
        out_shape=[
            jax.ShapeDtypeStruct((BATCH, SEQ, D_MODEL), F32),
            jax.ShapeDtypeStruct(tail_shape(POOL_PAST, D_POOL), F32),
            jax.ShapeDtypeStruct(tail_shape(SCONV_K - 1, D_SCONV), F32),
            jax.ShapeDtypeStruct(tail_shape(CCONV_K - 1, D_CCONV), F32),
        ] + cast_shape,
        scratch_shapes=[
            pltpu.VMEM((SUBLANES * POOL_PAST, D_POOL), F32),
            pltpu.VMEM((SUBLANES * (SCONV_K - 1), D_SCONV), F32),
            pltpu.VMEM((SUBLANES * (CCONV_K - 1), D_CCONV), F32),
        ],
        compiler_params=pltpu.CompilerParams(
            dimension_semantics=("arbitrary", "arbitrary"),
            vmem_limit_bytes=VMEM_LIMIT_BYTES),
        name="mix_prompt",
    )(x, mod, p["g_mix_pre"], p["g_mix_post"], w["w_in"], p["wpool"], p["pool_scale"],
      p["wl"], p["sconv_w"], p["cconv_w"], p["cconv_b"], p["cln_g"], p["cln_b"], w["w_out"],
      *(next_f32 or ()))


def _ffn_prompt_kernel(x_ref, mod_ref, gpre_ref, gpost_ref, wu_ref, fw_ref, wd_ref,
                       xo_ref, tup_ref, tail_up, act_buf):
    b = pl.program_id(0)
    t = pl.program_id(1)
    n_back = FFN_K - 1

    @pl.when(t == 0)
    def _():
        tail_up[...] = jnp.zeros(tail_up.shape, F32)

    x = x_ref[0]
    sh2 = mod_ref[pl.ds(b, 1), 3 * D_MODEL:4 * D_MODEL]
    sc2 = mod_ref[pl.ds(b, 1), 4 * D_MODEL:5 * D_MODEL]
    gt2 = mod_ref[pl.ds(b, 1), 5 * D_MODEL:6 * D_MODEL]
    hb = (_rms_scaled(x, gpre_ref[...] * (1.0 + sc2)) + sh2).astype(BF16)

    def conv(lo):
        cols = slice(lo, lo + FF_CHUNK)
        up = _dot(hb, wu_ref[:, cols])
        prev = tail_up[:, cols]
        outs = []
        for part in range(N_SUB):
            cur = up[part * SUB:(part + 1) * SUB]
            ext = jnp.concatenate([_halo_groups(prev, cur, n_back), cur], axis=0)
            prev = cur[SUB - SUBLANES * n_back:]
            out = fw_ref[FFN_K - 1:FFN_K, cols] * cur
            for k in range(n_back):
                out = out + fw_ref[k:k + 1, cols] * _back(ext, n_back, n_back - k)
            outs.append(out)
        tail_up[:, cols] = prev
        return jnp.concatenate(outs, axis=0)

    for j in range(N_FF_CHUNKS):
        lo = j * FF_CHUNK
        act_buf[:, lo:lo + FF_CHUNK] = (_silu(conv(lo)) * conv(D_FF + lo)).astype(BF16)
    ff = _dot(act_buf[...], wd_ref[...])

    xo_ref[0] = x + _rms_scaled(ff, gt2 * gpost_ref[...])

    @pl.when(t == N_SEQ_TILES - 1)
    def _():
        _last_positions(tup_ref, tail_up)


def _ffn_prompt_call(l, x, mod, p, w):
    tile = pl.BlockSpec((1, TM, D_MODEL), lambda b, t: (b, t, 0))
    ls = functools.partial(_layer_spec, l=l)
    once = dict(pipeline_mode=pl.Buffered(1))
    n_back = FFN_K - 1
    return pl.pallas_call(
        _ffn_prompt_kernel,
        grid=(BATCH, N_SEQ_TILES),
        in_specs=[
            tile,
            pl.BlockSpec((None, BATCH, 6 * D_MODEL), lambda b, t: (l, DEC_BATCH // BATCH, 0)),
            ls((1, D_MODEL)), ls((1, D_MODEL)),
            _whole_spec((D_MODEL, 2 * D_FF), **once),
            ls((FFN_K, 2 * D_FF)),
            _whole_spec((D_FF, D_MODEL), **once),
        ],
        out_specs=[tile, pl.BlockSpec((1, n_back, 2 * D_FF), lambda b, t: (b, 0, 0))],
        out_shape=[
            jax.ShapeDtypeStruct((BATCH, SEQ, D_MODEL), F32),
            jax.ShapeDtypeStruct((BATCH, n_back, 2 * D_FF), F32),
        ],
        scratch_shapes=[pltpu.VMEM((SUBLANES * n_back, 2 * D_FF), F32),
                        pltpu.VMEM((TM, D_FF), BF16)],
        compiler_params=pltpu.CompilerParams(
            dimension_semantics=("arbitrary", "arbitrary"),
            vmem_limit_bytes=VMEM_LIMIT_BYTES),
        name="ffn_prompt",
    )(x, mod, p["g_ffn_pre"], p["g_ffn_post"], w["w_up"], p["ffn_conv_w"], w["w_down"])


def _slab(a, t):
    return a[t * DEC_BATCH:(t + 1) * DEC_BATCH]


def _mix_dec_kernel(x_ref, mod_ref, gpre_ref, gpost_ref, win_ref, wpool_ref, pscale_ref,
                    wl_ref, sw_ref, cw_ref, cb_ref, lg_ref, lb_ref, wout_ref,
                    spool_ref, sz_ref, sglu_ref,
                    xo_ref, npool_ref, nz_ref, nglu_ref):
    x = x_ref[...]
    rep = lambda a: jnp.concatenate([a] * DEC_SEQ, axis=0)
    sh1 = rep(mod_ref[:, 0:D_MODEL])
    sc1 = rep(mod_ref[:, D_MODEL:2 * D_MODEL])
    gt1 = rep(mod_ref[:, 2 * D_MODEL:3 * D_MODEL])
    h = _rms_scaled(x, gpre_ref[...] * (1.0 + sc1)) + sh1
    u, z_b, bg, glu = _mixer_heads(_dot(h.astype(BF16), win_ref[...]))

    def ext(state_ref, cur, n_past, j):
        ch = cur.shape[1]
        return state_ref[:, j * ch:(j + 1) * ch] if j < n_past else _slab(cur, j - n_past)

    for new_ref, state_ref, cur in ((npool_ref, spool_ref, u), (nz_ref, sz_ref, z_b),
                                    (nglu_ref, sglu_ref, glu)):
        ch = cur.shape[1]
        n_past = state_ref.shape[1] // ch
        for k in range(n_past):
            new_ref[:, k * ch:(k + 1) * ch] = ext(state_ref, cur, n_past, k + DEC_SEQ)

    wins, convs_b, convs_c, cnts = [], [], [], []
    for t in range(DEC_SEQ):
        wins.append(_pool_windows(
            lambda i, cols: ext(spool_ref, u, POOL_PAST, POOL_PAST + t - i)[:, cols]))
        cnts.append(jnp.broadcast_to(jnp.minimum(F32(PAST_LEN + t + 1), wl_ref[...]),
                                     (DEC_BATCH, D_POOL)))
        acc = None
        for k in range(SCONV_K):
            term = sw_ref[k:k + 1, :] * ext(sz_ref, z_b, SCONV_K - 1, t + k)
            acc = term if acc is None else acc + term
        convs_b.append(acc)
        acc = None
        for k in range(CCONV_K):
            term = cw_ref[k:k + 1, :] * ext(sglu_ref, glu, CCONV_K - 1, t + k)
            acc = term if acc is None else acc + term
        convs_c.append(acc)
    cat = lambda xs: jnp.concatenate(xs, axis=0)
    xo_ref[...] = _mixer_out(x, gt1, gpost_ref, cat(wins), cat(cnts), u, cat(convs_b), bg,
                             cat(convs_c), wpool_ref, pscale_ref, cb_ref, lg_ref, lb_ref,
                             wout_ref)


def _full_spec(shape):
    return pl.BlockSpec(shape, lambda i: (0,) * len(shape))


def _mix_dec_call(l, x, mod, spool, sz, sglu, p, w):
    ls = functools.partial(_layer_spec, l=l)
    states = (spool, sz, sglu)
    return pl.pallas_call(
        _mix_dec_kernel,
        grid=(1,),
        in_specs=[
            _full_spec((DEC_ROWS, D_MODEL)), ls((DEC_BATCH, 6 * D_MODEL)),
            ls((1, D_MODEL)), ls((1, D_MODEL)),
            _whole_spec((D_MODEL, D_IN)),
            ls((D_POOL, D_POOL)), ls((1, D_POOL)),
            _full_spec((1, D_POOL)),
            ls((SCONV_K, D_SCONV)),
            ls((CCONV_K, D_CCONV)), ls((1, D_CCONV)), ls((1, D_CCONV)), ls((1, D_CCONV)),
            _whole_spec((D_MODEL, D_MODEL)),
        ] + [ls(s.shape[1:]) for s in states],
        out_specs=[_full_spec((DEC_ROWS, D_MODEL))] + [_full_spec(s.shape[1:]) for s in states],
        out_shape=[jax.ShapeDtypeStruct((DEC_ROWS, D_MODEL), F32)]
        + [jax.ShapeDtypeStruct(s.shape[1:], F32) for s in states],
        compiler_params=pltpu.CompilerParams(
            dimension_semantics=("arbitrary",), vmem_limit_bytes=VMEM_LIMIT_BYTES),
        name="mix_dec",
    )(x, mod, p["g_mix_pre"], p["g_mix_post"], w["w_in"], p["wpool"], p["pool_scale"],
      p["wl"], p["sconv_w"], p["cconv_w"], p["cconv_b"], p["cln_g"], p["cln_b"],
      w["w_out"], *states)


def _ffn_dec_kernel(x_ref, mod_ref, gpre_ref, gpost_ref, wu_ref, fw_ref, wd_ref, sup_ref,
                    xo_ref, up_ref, act_buf):
    x = x_ref[...]
    rep = lambda a: jnp.concatenate([a] * DEC_SEQ, axis=0)
    sh2 = rep(mod_ref[:, 3 * D_MODEL:4 * D_MODEL])
    sc2 = rep(mod_ref[:, 4 * D_MODEL:5 * D_MODEL])
    gt2 = rep(mod_ref[:, 5 * D_MODEL:6 * D_MODEL])
    hb = (_rms_scaled(x, gpre_ref[...] * (1.0 + sc2)) + sh2).astype(BF16)
    n_past = FFN_K - 1

    def conv(lo):
        cols = slice(lo, lo + FF_CHUNK)
        cur = _dot(hb, wu_ref[:, cols])
        state_cols = lambda k: slice(k * 2 * D_FF + lo, k * 2 * D_FF + lo + FF_CHUNK)
        for k in range(n_past):
            up_ref[:, state_cols(k)] = _slab(cur, DEC_SEQ - n_past + k)
        outs = []
        for t in range(DEC_SEQ):
            acc = None
            for k in range(FFN_K):
                j = t + k
                row = sup_ref[:, state_cols(j)] if j < n_past else _slab(cur, j - n_past)
                term = fw_ref[k:k + 1, cols] * row
                acc = term if acc is None else acc + term
            outs.append(acc)
        return jnp.concatenate(outs, axis=0)

    for j in range(N_FF_CHUNKS):
        lo = j * FF_CHUNK
        act_buf[:, lo:lo + FF_CHUNK] = (_silu(conv(lo)) * conv(D_FF + lo)).astype(BF16)
    ff = _dot(act_buf[...], wd_ref[...])
    xo_ref[...] = x + _rms_scaled(ff, gt2 * gpost_ref[...])


def _ffn_dec_call(l, x, mod, sup, p, w):
    ls = functools.partial(_layer_spec, l=l)
    return pl.pallas_call(
        _ffn_dec_kernel,
        grid=(1,),
        in_specs=[
            _full_spec((DEC_ROWS, D_MODEL)), ls((DEC_BATCH, 6 * D_MODEL)),
            ls((1, D_MODEL)), ls((1, D_MODEL)),
            _whole_spec((D_MODEL, 2 * D_FF)),
            ls((FFN_K, 2 * D_FF)),
            _whole_spec((D_FF, D_MODEL)),
            ls(sup.shape[1:]),
        ],
        out_specs=[_full_spec((DEC_ROWS, D_MODEL)), _full_spec(sup.shape[1:])],
        out_shape=[
            jax.ShapeDtypeStruct((DEC_ROWS, D_MODEL), F32),
            jax.ShapeDtypeStruct(sup.shape[1:], F32),
        ],
        scratch_shapes=[pltpu.VMEM((DEC_ROWS, D_FF), BF16)],
        compiler_params=pltpu.CompilerParams(
            dimension_semantics=("arbitrary",), vmem_limit_bytes=VMEM_LIMIT_BYTES),
        name="ffn_dec",
    )(x, mod, p["g_ffn_pre"], p["g_ffn_post"], w["w_up"], p["ffn_conv_w"], w["w_down"], sup)


def _block_diag(pool_w):
    g = len(POOL_WINDOWS)
    eye = jnp.eye(g, dtype=bool)[None, :, None, :, None]
    full = jnp.where(eye, pool_w[:, :, :, None, :], 0.0)
    return full.reshape(DEPTH, D_POOL, D_POOL)


def _interleave_glu_columns(w_in):
    parts = [w_in[..., :O_AC]]
    for q in range(N_GLU_TILES):
        parts.append(w_in[..., O_AC + q * LANES:O_AC + (q + 1) * LANES])
        parts.append(w_in[..., O_BC + q * LANES:O_BC + (q + 1) * LANES])
    return jnp.concatenate(parts, axis=-1)


def _tile_permute(x, inverse=False):
    shape = (BATCH, SEQ // SUB) + ((SEG, SUBLANES) if inverse else (SUBLANES, SEG)) + (D_MODEL,)
    return x.reshape(shape).swapaxes(2, 3).reshape(BATCH, SEQ, D_MODEL)


def kernel(x_prompt, x_sample, c_prompt, c_sample, state_pool, state_sconv, state_cconv, state_ffn,
           w_ada, b_ada, g_mix_pre, g_mix_post, g_ffn_pre, g_ffn_post, w_in, pool_w, pool_scale,
           sconv_w, cconv_w, cconv_b, cln_g, cln_b, w_out, w_up, ffn_conv_w, w_down):
    mod = _ada_call(jnp.concatenate([c_sample, c_prompt], axis=0), w_ada, b_ada)
    row = lambda a: a[:, None, :]
    p = dict(
        g_mix_pre=row(g_mix_pre), g_mix_post=row(g_mix_post),
        g_ffn_pre=row(g_ffn_pre), g_ffn_post=row(g_ffn_post),
        wpool=_block_diag(pool_w).astype(BF16),
        pool_scale=row(pool_scale),
        wl=jnp.repeat(jnp.asarray(POOL_WINDOWS, F32), POOL_GC)[None, :],
        sconv_w=sconv_w, cconv_w=cconv_w, cconv_b=row(cconv_b),
        cln_g=row(cln_g), cln_b=row(cln_b),
        ffn_conv_w=ffn_conv_w,
    )
    names = ("w_in", "w_out", "w_up", "w_down")
    stacked = (w_in, w_out, w_up, w_down)
    w = dict(zip(names, (_interleave_glu_columns(w_in[0]).astype(BF16), w_out[0].astype(BF16),
                         w_up[0].astype(BF16), w_down[0].astype(BF16))))
    dec_states = (state_pool, state_sconv, state_cconv, state_ffn)
    spool, sz, sglu, sup = (s.reshape(DEPTH, DEC_BATCH, -1) for s in dec_states)

    xp = _tile_permute(x_prompt)
    xs = jnp.swapaxes(x_sample, 0, 1).reshape(DEC_ROWS, D_MODEL)
    tails = [[] for _ in range(4)]
    news = [[] for _ in range(4)]
    for l in range(DEPTH):
        xp, tpool, tz, tglu, *w_next = _mix_prompt_call(
            l, xp, mod, p, w, stacked if l + 1 < DEPTH else None)
        xp, tup = _ffn_prompt_call(l, xp, mod, p, w)
        for acc, a in zip(tails, (tpool, tz, tglu, tup)):
            acc.append(a)
        xs, npool, nz, nglu = _mix_dec_call(l, xs, mod, spool, sz, sglu, p, w)
        xs, nup = _ffn_dec_call(l, xs, mod, sup, p, w)
        w = dict(zip(names, w_next))
        for acc, a in zip(news, (npool, nz, nglu, nup)):
            acc.append(a)

    prompt_state = [jnp.stack(a) for a in tails]
    dec_state = [jnp.stack(a).reshape(s.shape) for a, s in zip(news, dec_states)]
    ys = jnp.swapaxes(xs.reshape(DEC_SEQ, DEC_BATCH, D_MODEL), 0, 1)
    return (_tile_permute(xp, inverse=True), ys) + tuple(prompt_state) + tuple(dec_state)
```

```python
import functools

import jax
import jax.numpy as jnp
from jax import lax
from jax.experimental import pallas as pl
from jax.experimental.pallas import tpu as pltpu

D_MODEL = 1024
BATCH = 8
SEQ = 2048
DEPTH = 4
DEC_BATCH = 128
DEC_SEQ = 4
PAST_LEN = 16384

D_POOL = 256
POOL_WINDOWS = (2, 4, 8, 16)
POOL_GC = 64
POOL_PAST = 15
D_SCONV = 384
D_CCONV = 384
SCONV_K = 3
CCONV_K = 31
FFN_K = 3
D_FF = 2816
D_IN = D_POOL + 3 * D_SCONV + 2 * D_CCONV
EPS = 1e-6

LANES = 128
SUBLANES = 8

O_HB = D_POOL
O_BG = O_HB + D_SCONV
O_CG = O_BG + D_SCONV
O_AC = O_CG + D_SCONV
O_BC = O_AC + D_CCONV
O_GLU = O_AC
N_GLU_TILES = D_CCONV // LANES

FF_CHUNK = 256
N_FF_CHUNKS = D_FF // FF_CHUNK
TM = 1024
N_SEQ_TILES = SEQ // TM
TM_MIX = 1024
MIX_TILES = SEQ // TM_MIX
SUB = 512
N_SUB = TM // SUB
MIX_SUBS = TM_MIX // SUB
SEG = SUB // SUBLANES
DEC_ROWS = DEC_BATCH * DEC_SEQ
N_ADA = BATCH + DEC_BATCH
ADA_TN = 1024
VMEM_LIMIT_BYTES = 56 * 1024 * 1024

F32 = jnp.float32
BF16 = jnp.bfloat16

assert 2 * POOL_GC == LANES and len(POOL_WINDOWS) == 4
assert SEG >= CCONV_K - 1


def _rms_scaled(x, scale):
    return x * lax.rsqrt(jnp.mean(x * x, axis=-1, keepdims=True) + EPS) * scale


def _layer_norm(x, g, b):
    mu = jnp.mean(x, axis=-1, keepdims=True)
    xc = x - mu
    var = jnp.mean(xc * xc, axis=-1, keepdims=True)
    return xc * lax.rsqrt(var + EPS) * g + b


def _silu(x):
    return x * jax.nn.sigmoid(x)


def _dot(a, b):
    return jnp.dot(a, b, preferred_element_type=F32)


def _layer_spec(shape, l, **kw):
    return pl.BlockSpec((None,) + shape, lambda *_: (l,) + (0,) * len(shape), **kw)


def _whole_spec(shape, **kw):
    return pl.BlockSpec(shape, lambda *_: (0,) * len(shape), **kw)


def _ada_kernel(c_ref, w_ref, b_ref, o_ref):
    s = _silu(c_ref[...]).astype(BF16)
    o_ref[...] = _dot(s, w_ref[...].astype(BF16)) + b_ref[...]


def _ada_call(c_all, w_ada, b_ada):
    return pl.pallas_call(
        _ada_kernel,
        grid=(DEPTH, 6 * D_MODEL // ADA_TN),
        in_specs=[
            pl.BlockSpec((N_ADA, D_MODEL), lambda l, n: (0, 0)),
            pl.BlockSpec((None, D_MODEL, ADA_TN), lambda l, n: (l, 0, n)),
            pl.BlockSpec((None, 1, ADA_TN), lambda l, n: (l, 0, n)),
        ],
        out_specs=pl.BlockSpec((None, N_ADA, ADA_TN), lambda l, n: (l, 0, n)),
        out_shape=jax.ShapeDtypeStruct((DEPTH, N_ADA, 6 * D_MODEL), F32),
        compiler_params=pltpu.CompilerParams(
            dimension_semantics=("arbitrary", "arbitrary"),
            vmem_limit_bytes=VMEM_LIMIT_BYTES),
        name="ada",
    )(c_all, w_ada, b_ada.reshape(DEPTH, 1, 6 * D_MODEL))


def _mixer_heads(proj):
    u = proj[:, 0:O_HB]
    z_b = proj[:, O_CG:O_AC] * proj[:, O_HB:O_BG]
    bg = proj[:, O_BG:O_CG]
    glu = []
    for q in range(N_GLU_TILES):
        a = proj[:, O_GLU + 2 * q * LANES:O_GLU + (2 * q + 1) * LANES]
        b = proj[:, O_GLU + (2 * q + 1) * LANES:O_GLU + (2 * q + 2) * LANES]
        glu.append(a * jax.nn.sigmoid(b))
    return u, z_b, bg, jnp.concatenate(glu, axis=-1)


def _pool_windows(tap):
    halves = []
    for h in range(2):
        w_small, w_big = POOL_WINDOWS[2 * h], POOL_WINDOWS[2 * h + 1]
        cols = slice(h * LANES, (h + 1) * LANES)
        s = tap(0, cols)
        for i in range(1, w_small):
            s = s + tap(i, cols)
        small = s
        for i in range(w_small, w_big):
            s = s + tap(i, cols)
        lane = lax.broadcasted_iota(jnp.int32, s.shape, 1)
        halves.append(jnp.where(lane < POOL_GC, small, s))
    return jnp.concatenate(halves, axis=-1)


def _mixer_out(x, gt1, gpost_ref, win, cnt, u, conv_b, bg, conv_c,
               wpool_ref, pscale_ref, cb_ref, lg_ref, lb_ref, wout_ref):
    pooled = (win / cnt - u).astype(BF16)
    y_a = (_dot(pooled, wpool_ref[...]) * pscale_ref[...]).astype(BF16)
    y_b = (bg * conv_b).astype(BF16)
    y_c = _silu(_layer_norm(conv_c + cb_ref[...], lg_ref[...], lb_ref[...])).astype(BF16)
    mix = _dot(jnp.concatenate([y_a, y_b, y_c], axis=-1), wout_ref[...])
    return x + _rms_scaled(mix, gt1 * gpost_ref[...])


def _halo_groups(prev_tail, cur, n_back):
    tail = cur[SUB - SUBLANES * n_back:]
    sub = lax.broadcasted_iota(jnp.int32, tail.shape, 0) & (SUBLANES - 1)
    mixed = jnp.where(sub == SUBLANES - 1, prev_tail, tail)
    groups = [pltpu.roll(mixed[SUBLANES * i:SUBLANES * (i + 1)], 1, axis=0) for i in range(n_back)]
    return jnp.concatenate(groups, axis=0)


def _last_positions(out_ref, tail_ref):
    for k in range(out_ref.shape[1]):
        row = SUBLANES * k + SUBLANES - 1
        out_ref[0, k:k + 1, :] = tail_ref[row:row + 1, :]


def _back(ext, n_back, j):
    lo = SUBLANES * (n_back - j)
    return ext[lo:lo + SUB]


def _mix_prompt_kernel(x_ref, mod_ref, gpre_ref, gpost_ref, win_ref, wpool_ref, pscale_ref,
                       wl_ref, sw_ref, cw_ref, cb_ref, lg_ref, lb_ref, wout_ref, *rest,
                       cast_next):
    b = pl.program_id(0)
    t = pl.program_id(1)
    if cast_next:
        next_f32, rest = rest[:4], rest[4:]
        xo_ref, tpool_ref, tz_ref, tglu_ref = rest[:4]
        next_bf16, rest = rest[4:8], rest[8:]
        next_bf16[0][...] = _interleave_glu_columns(next_f32[0][...]).astype(BF16)
        for src, dst in zip(next_f32[1:], next_bf16[1:]):
            dst[...] = src[...].astype(BF16)
    else:
        xo_ref, tpool_ref, tz_ref, tglu_ref = rest[:4]
        rest = rest[4:]
    tail_u, tail_z, tail_glu = rest

    @pl.when(t == 0)
    def _():
        tail_u[...] = jnp.zeros(tail_u.shape, F32)
        tail_z[...] = jnp.zeros(tail_z.shape, F32)
        tail_glu[...] = jnp.zeros(tail_glu.shape, F32)

    sh1 = mod_ref[pl.ds(b, 1), 0:D_MODEL]
    sc1 = mod_ref[pl.ds(b, 1), D_MODEL:2 * D_MODEL]
    gt1 = mod_ref[pl.ds(b, 1), 2 * D_MODEL:3 * D_MODEL]
    pre_scale = gpre_ref[...] * (1.0 + sc1)

    xs, heads = [], []
    for part in range(MIX_SUBS):
        x = x_ref[0, part * SUB:(part + 1) * SUB, :]
        h = _rms_scaled(x, pre_scale) + sh1
        xs.append(x)
        heads.append(_mixer_heads(_dot(h.astype(BF16), win_ref[...])))

    for part in range(MIX_SUBS):
        rows = slice(part * SUB, (part + 1) * SUB)
        x = xs[part]
        u, z_b, bg, glu = heads[part]

        eu = jnp.concatenate([_halo_groups(tail_u[...], u, POOL_PAST), u], axis=0)
        ez = jnp.concatenate([_halo_groups(tail_z[...], z_b, SCONV_K - 1), z_b], axis=0)
        eg = jnp.concatenate([_halo_groups(tail_glu[...], glu, CCONV_K - 1), glu], axis=0)
        tail_u[...] = u[SUB - SUBLANES * POOL_PAST:]
        tail_z[...] = z_b[SUB - SUBLANES * (SCONV_K - 1):]
        tail_glu[...] = glu[SUB - SUBLANES * (CCONV_K - 1):]

        win = _pool_windows(lambda i, cols: _back(eu, POOL_PAST, i)[:, cols])
        row = lax.broadcasted_iota(jnp.int32, (SUB, D_POOL), 0)
        pos1 = (t * MIX_SUBS + part) * SUB + 1 + (row & (SUBLANES - 1)) * SEG + (row >> 3)
        cnt = jnp.minimum(pos1.astype(F32), wl_ref[...])

        conv_b = sw_ref[SCONV_K - 1:SCONV_K, :] * z_b
        for k in range(SCONV_K - 1):
            conv_b = conv_b + sw_ref[k:k + 1, :] * _back(ez, SCONV_K - 1, SCONV_K - 1 - k)
        conv_c = cw_ref[CCONV_K - 1:CCONV_K, :] * glu
        for k in range(CCONV_K - 1):
            conv_c = conv_c + cw_ref[k:k + 1, :] * _back(eg, CCONV_K - 1, CCONV_K - 1 - k)

        xo_ref[0, rows, :] = _mixer_out(x, gt1, gpost_ref, win, cnt, u, conv_b, bg, conv_c,
                                        wpool_ref, pscale_ref, cb_ref, lg_ref, lb_ref, wout_ref)

    @pl.when(t == MIX_TILES - 1)
    def _():
        _last_positions(tpool_ref, tail_u)
        _last_positions(tz_ref, tail_z)
        _last_positions(tglu_ref, tail_glu)


def _mix_prompt_call(l, x, mod, p, w, next_f32):
    tile = pl.BlockSpec((1, TM_MIX, D_MODEL), lambda b, t: (b, t, 0))
    tail = lambda n_back, ch: pl.BlockSpec((1, n_back, ch), lambda b, t: (b, 0, 0))
    tail_shape = lambda n_back, ch: (BATCH, n_back, ch)
    once = dict(pipeline_mode=pl.Buffered(1))
    ls = functools.partial(_layer_spec, l=l)
    n_steps = BATCH * MIX_TILES
    cast_in, cast_out, cast_shape = [], [], []
    for a in next_f32 or ():
        rows, cols = a.shape[1] // n_steps, a.shape[2]
        cast_in.append(pl.BlockSpec((None, rows, cols),
                                    lambda b, t: (l + 1, b * MIX_TILES + t, 0)))
        cast_out.append(pl.BlockSpec((rows, cols), lambda b, t: (b * MIX_TILES + t, 0)))
        cast_shape.append(jax.ShapeDtypeStruct(a.shape[1:], BF16))
    return pl.pallas_call(
        functools.partial(_mix_prompt_kernel, cast_next=bool(next_f32)),
        grid=(BATCH, MIX_TILES),
        in_specs=[
            tile,
            pl.BlockSpec((None, BATCH, 6 * D_MODEL), lambda b, t: (l, DEC_BATCH // BATCH, 0)),
            ls((1, D_MODEL)), ls((1, D_MODEL)),
            _whole_spec((D_MODEL, D_IN), **once),
            ls((D_POOL, D_POOL)), ls((1, D_POOL)),
            pl.BlockSpec((1, D_POOL), lambda b, t: (0, 0)),
            ls((SCONV_K, D_SCONV)),
            ls((CCONV_K, D_CCONV)), ls((1, D_CCONV)), ls((1, D_CCONV)), ls((1, D_CCONV)),
            _whole_spec((D_MODEL, D_MODEL), **once),
        ] + cast_in,
        out_specs=[tile, tail(POOL_PAST, D_POOL), tail(SCONV_K - 1, D_SCONV),
                   tail(CCONV_K - 1, D_CCONV)] + cast_out,
        out_shape=[
            jax.ShapeDtypeStruct((BATCH, SEQ, D_MODEL), F32),
            jax.ShapeDtypeStruct(tail_shape(POOL_PAST, D_POOL), F32),
            jax.ShapeDtypeStruct(tail_shape(SCONV_K - 1, D_SCONV), F32),
            jax.ShapeDtypeStruct(tail_shape(CCONV_K - 1, D_CCONV), F32),
        ] + cast_shape,
        scratch_shapes=[
            pltpu.VMEM((SUBLANES * POOL_PAST, D_POOL), F32),
            pltpu.VMEM((SUBLANES * (SCONV_K - 1), D_SCONV), F32),
            pltpu.VMEM((SUBLANES * (CCONV_K - 1), D_CCONV), F32),
        ],
        compiler_params=pltpu.CompilerParams(
            dimension_semantics=("arbitrary", "arbitrary"),
            vmem_limit_bytes=VMEM_LIMIT_BYTES),
        name="mix_prompt",
    )(x, mod, p["g_mix_pre"], p["g_mix_post"], w["w_in"], p["wpool"], p["pool_scale"],
      p["wl"], p["sconv_w"], p["cconv_w"], p["cconv_b"], p["cln_g"], p["cln_b"], w["w_out"],
      *(next_f32 or ()))


def _ffn_prompt_kernel(x_ref, mod_ref, gpre_ref, gpost_ref, wu_ref, fw_ref, wd_ref,
                       xo_ref, tup_ref, tail_up, act_buf):
    b = pl.program_id(0)
    t = pl.program_id(1)
    n_back = FFN_K - 1

    @pl.when(t == 0)
    def _():
        tail_up[...] = jnp.zeros(tail_up.shape, F32)

    x = x_ref[0]
    sh2 = mod_ref[pl.ds(b, 1), 3 * D_MODEL:4 * D_MODEL]
    sc2 = mod_ref[pl.ds(b, 1), 4 * D_MODEL:5 * D_MODEL]
    gt2 = mod_ref[pl.ds(b, 1), 5 * D_MODEL:6 * D_MODEL]
    hb = (_rms_scaled(x, gpre_ref[...] * (1.0 + sc2)) + sh2).astype(BF16)

    def conv(lo):
        cols = slice(lo, lo + FF_CHUNK)
        up = _dot(hb, wu_ref[:, cols])
        prev = tail_up[:, cols]
        outs = []
        for part in range(N_SUB):
            cur = up[part * SUB:(part + 1) * SUB]
            ext = jnp.concatenate([_halo_groups(prev, cur, n_back), cur], axis=0)
            prev = cur[SUB - SUBLANES * n_back:]
            out = fw_ref[FFN_K - 1:FFN_K, cols] * cur
            for k in range(n_back):
                out = out + fw_ref[k:k + 1, cols] * _back(ext, n_back, n_back - k)
            outs.append(out)
        tail_up[:, cols] = prev
        return jnp.concatenate(outs, axis=0)

    for j in range(N_FF_CHUNKS):
        lo = j * FF_CHUNK
        act_buf[:, lo:lo + FF_CHUNK] = (_silu(conv(lo)) * conv(D_FF + lo)).astype(BF16)
    ff = _dot(act_buf[...], wd_ref[...])

    xo_ref[0] = x + _rms_scaled(ff, gt2 * gpost_ref[...])

    @pl.when(t == N_SEQ_TILES - 1)
    def _():
        _last_positions(tup_ref, tail_up)


def _ffn_prompt_call(l, x, mod, p, w):
    tile = pl.BlockSpec((1, TM, D_MODEL), lambda b, t: (b, t, 0))
    ls = functools.partial(_layer_spec, l=l)
    once = dict(pipeline_mode=pl.Buffered(1))
    n_back = FFN_K - 1
    return pl.pallas_call(
        _ffn_prompt_kernel,
        grid=(BATCH, N_SEQ_TILES),
        in_specs=[
            tile,
            pl.BlockSpec((None, BATCH, 6 * D_MODEL), lambda b, t: (l, DEC_BATCH // BATCH, 0)),
            ls((1, D_MODEL)), ls((1, D_MODEL)),
            _whole_spec((D_MODEL, 2 * D_FF), **once),
            ls((FFN_K, 2 * D_FF)),
            _whole_spec((D_FF, D_MODEL), **once),
        ],
        out_specs=[tile, pl.BlockSpec((1, n_back, 2 * D_FF), lambda b, t: (b, 0, 0))],
        out_shape=[
            jax.ShapeDtypeStruct((BATCH, SEQ, D_MODEL), F32),
            jax.ShapeDtypeStruct((BATCH, n_back, 2 * D_FF), F32),
        ],
        scratch_shapes=[pltpu.VMEM((SUBLANES * n_back, 2 * D_FF), F32),
                        pltpu.VMEM((TM, D_FF), BF16)],
        compiler_params=pltpu.CompilerParams(
            dimension_semantics=("arbitrary", "arbitrary"),
            vmem_limit_bytes=VMEM_LIMIT_BYTES),
        name="ffn_prompt",
    )(x, mod, p["g_ffn_pre"], p["g_ffn_post"], w["w_up"], p["ffn_conv_w"], w["w_down"])


def _slab(a, t):
    return a[t * DEC_BATCH:(t + 1) * DEC_BATCH]


def _mix_dec_kernel(x_ref, mod_ref, gpre_ref, gpost_ref, win_ref, wpool_ref, pscale_ref,
                    wl_ref, sw_ref, cw_ref, cb_ref, lg_ref, lb_ref, wout_ref,
                    spool_ref, sz_ref, sglu_ref,
                    xo_ref, npool_ref, nz_ref, nglu_ref):
    x = x_ref[...]
    rep = lambda a: jnp.concatenate([a] * DEC_SEQ, axis=0)
    sh1 = rep(mod_ref[:, 0:D_MODEL])
    sc1 = rep(mod_ref[:, D_MODEL:2 * D_MODEL])
    gt1 = rep(mod_ref[:, 2 * D_MODEL:3 * D_MODEL])
    h = _rms_scaled(x, gpre_ref[...] * (1.0 + sc1)) + sh1
    u, z_b, bg, glu = _mixer_heads(_dot(h.astype(BF16), win_ref[...]))

    def ext(state_ref, cur, n_past, j):
        return state_ref[j] if j < n_past else _slab(cur, j - n_past)

    for new_ref, state_ref, cur in ((npool_ref, spool_ref, u), (nz_ref, sz_ref, z_b),
                                    (nglu_ref, sglu_ref, glu)):
        n_past = state_ref.shape[0]
        for k in range(n_past):
            new_ref[k] = ext(state_ref, cur, n_past, k + DEC_SEQ)

    wins, convs_b, convs_c, cnts = [], [], [], []
    for t in range(DEC_SEQ):
        wins.append(_pool_windows(
            lambda i, cols: ext(spool_ref, u, POOL_PAST, POOL_PAST + t - i)[:, cols]))
        cnts.append(jnp.broadcast_to(jnp.minimum(F32(PAST_LEN + t + 1), wl_ref[...]),
                                     (DEC_BATCH, D_POOL)))
        acc = None
        for k in range(SCONV_K):
            term = sw_ref[k:k + 1, :] * ext(sz_ref, z_b, SCONV_K - 1, t + k)
            acc = term if acc is None else acc + term
        convs_b.append(acc)
        acc = None
        for k in range(CCONV_K):
            term = cw_ref[k:k + 1, :] * ext(sglu_ref, glu, CCONV_K - 1, t + k)
            acc = term if acc is None else acc + term
        convs_c.append(acc)
    cat = lambda xs: jnp.concatenate(xs, axis=0)
    xo_ref[...] = _mixer_out(x, gt1, gpost_ref, cat(wins), cat(cnts), u, cat(convs_b), bg,
                             cat(convs_c), wpool_ref, pscale_ref, cb_ref, lg_ref, lb_ref,
                             wout_ref)


def _full_spec(shape):
    return pl.BlockSpec(shape, lambda i: (0,) * len(shape))


def _mix_dec_call(l, x, mod, spool, sz, sglu, p, w):
    ls = functools.partial(_layer_spec, l=l)
    states = (spool, sz, sglu)
    return pl.pallas_call(
        _mix_dec_kernel,
        grid=(1,),
        in_specs=[
            _full_spec((DEC_ROWS, D_MODEL)), ls((DEC_BATCH, 6 * D_MODEL)),
            ls((1, D_MODEL)), ls((1, D_MODEL)),
            _whole_spec((D_MODEL, D_IN)),
            ls((D_POOL, D_POOL)), ls((1, D_POOL)),
            _full_spec((1, D_POOL)),
            ls((SCONV_K, D_SCONV)),
            ls((CCONV_K, D_CCONV)), ls((1, D_CCONV)), ls((1, D_CCONV)), ls((1, D_CCONV)),
            _whole_spec((D_MODEL, D_MODEL)),
        ] + [ls(s.shape[1:]) for s in states],
        out_specs=[_full_spec((DEC_ROWS, D_MODEL))] + [_full_spec(s.shape[1:]) for s in states],
        out_shape=[jax.ShapeDtypeStruct((DEC_ROWS, D_MODEL), F32)]
        + [jax.ShapeDtypeStruct(s.shape[1:], F32) for s in states],
        compiler_params=pltpu.CompilerParams(
            dimension_semantics=("arbitrary",), vmem_limit_bytes=VMEM_LIMIT_BYTES),
        name="mix_dec",
    )(x, mod, p["g_mix_pre"], p["g_mix_post"], w["w_in"], p["wpool"], p["pool_scale"],
      p["wl"], p["sconv_w"], p["cconv_w"], p["cconv_b"], p["cln_g"], p["cln_b"],
      w["w_out"], *states)


def _ffn_dec_kernel(x_ref, mod_ref, gpre_ref, gpost_ref, wg_ref, wv_ref, fwg_ref, fwv_ref, wd_ref,
                    sg_ref, sv_ref, xo_ref, ug_ref, uv_ref, hb_buf, ff_acc):
    j = pl.program_id(0)
    rep = lambda a: jnp.concatenate([a] * DEC_SEQ, axis=0)
    n_past = FFN_K - 1

    @pl.when(j == 0)
    def _():
        sh2 = rep(mod_ref[:, 3 * D_MODEL:4 * D_MODEL])
        sc2 = rep(mod_ref[:, 4 * D_MODEL:5 * D_MODEL])
        h = _rms_scaled(x_ref[...], gpre_ref[...] * (1.0 + sc2)) + sh2
        hb_buf[...] = h.astype(BF16)
        ff_acc[...] = jnp.zeros(ff_acc.shape, F32)

    hb = hb_buf[...]

    def conv(w_ref, fw_ref, state_ref, new_ref):
        cur = _dot(hb, w_ref[...])
        for k in range(n_past):
            new_ref[k] = _slab(cur, DEC_SEQ - n_past + k)
        outs = []
        for t in range(DEC_SEQ):
            acc = None
            for k in range(FFN_K):
                i = t + k
                row = state_ref[i] if i < n_past else _slab(cur, i - n_past)
                term = fw_ref[k:k + 1, :] * row
                acc = term if acc is None else acc + term
            outs.append(acc)
        return jnp.concatenate(outs, axis=0)

    act = _silu(conv(wg_ref, fwg_ref, sg_ref, ug_ref)) * conv(wv_ref, fwv_ref, sv_ref, uv_ref)
    ff_acc[...] += _dot(act.astype(BF16), wd_ref[...])

    @pl.when(j == N_FF_CHUNKS - 1)
    def _():
        gt2 = rep(mod_ref[:, 5 * D_MODEL:6 * D_MODEL])
        xo_ref[...] = x_ref[...] + _rms_scaled(ff_acc[...], gt2 * gpost_ref[...])


def _ffn_dec_call(l, x, mod, sup, p, w):
    n_past = FFN_K - 1
    const = lambda shape: pl.BlockSpec(shape, lambda j: (0,) * len(shape))
    ls = functools.partial(_layer_spec, l=l)
    gate = lambda j: j
    val = lambda j: N_FF_CHUNKS + j
    up_w = lambda half: pl.BlockSpec((D_MODEL, FF_CHUNK), lambda j: (0, half(j)))
    conv_w = lambda half: pl.BlockSpec((None, FFN_K, FF_CHUNK), lambda j: (l, 0, half(j)))
    state = lambda half: pl.BlockSpec((None, n_past, DEC_BATCH, FF_CHUNK),
                                      lambda j: (l, 0, 0, half(j)))
    new_state = pl.BlockSpec((n_past, DEC_BATCH, FF_CHUNK), lambda j: (0, 0, j))
    return pl.pallas_call(
        _ffn_dec_kernel,
        grid=(N_FF_CHUNKS,),
        in_specs=[
            const((DEC_ROWS, D_MODEL)), ls((DEC_BATCH, 6 * D_MODEL)),
            ls((1, D_MODEL)), ls((1, D_MODEL)),
            up_w(gate), up_w(val), conv_w(gate), conv_w(val),
            pl.BlockSpec((FF_CHUNK, D_MODEL), lambda j: (j, 0)),
            state(gate), state(val),
        ],
        out_specs=[const((DEC_ROWS, D_MODEL)), new_state, new_state],
        out_shape=[
            jax.ShapeDtypeStruct((DEC_ROWS, D_MODEL), F32),
            jax.ShapeDtypeStruct((n_past, DEC_BATCH, D_FF), F32),
            jax.ShapeDtypeStruct((n_past, DEC_BATCH, D_FF), F32),
        ],
        scratch_shapes=[pltpu.VMEM((DEC_ROWS, D_MODEL), BF16),
                        pltpu.VMEM((DEC_ROWS, D_MODEL), F32)],
        compiler_params=pltpu.CompilerParams(
            dimension_semantics=("arbitrary",), vmem_limit_bytes=VMEM_LIMIT_BYTES),
        name="ffn_dec",
    )(x, mod, p["g_ffn_pre"], p["g_ffn_post"], w["w_up"], w["w_up"], p["ffn_conv_w"],
      p["ffn_conv_w"], w["w_down"], sup, sup)


def _block_diag(pool_w):
    g = len(POOL_WINDOWS)
    eye = jnp.eye(g, dtype=bool)[None, :, None, :, None]
    full = jnp.where(eye, pool_w[:, :, :, None, :], 0.0)
    return full.reshape(DEPTH, D_POOL, D_POOL)


def _interleave_glu_columns(w_in):
    parts = [w_in[..., :O_AC]]
    for q in range(N_GLU_TILES):
        parts.append(w_in[..., O_AC + q * LANES:O_AC + (q + 1) * LANES])
        parts.append(w_in[..., O_BC + q * LANES:O_BC + (q + 1) * LANES])
    return jnp.concatenate(parts, axis=-1)


def _tile_permute(x, inverse=False):
    shape = (BATCH, SEQ // SUB) + ((SEG, SUBLANES) if inverse else (SUBLANES, SEG)) + (D_MODEL,)
    return x.reshape(shape).swapaxes(2, 3).reshape(BATCH, SEQ, D_MODEL)


def kernel(x_prompt, x_sample, c_prompt, c_sample, state_pool, state_sconv, state_cconv, state_ffn,
           w_ada, b_ada, g_mix_pre, g_mix_post, g_ffn_pre, g_ffn_post, w_in, pool_w, pool_scale,
           sconv_w, cconv_w, cconv_b, cln_g, cln_b, w_out, w_up, ffn_conv_w, w_down):
    mod = _ada_call(jnp.concatenate([c_sample, c_prompt], axis=0), w_ada, b_ada)
    row = lambda a: a[:, None, :]
    p = dict(
        g_mix_pre=row(g_mix_pre), g_mix_post=row(g_mix_post),
        g_ffn_pre=row(g_ffn_pre), g_ffn_post=row(g_ffn_post),
        wpool=_block_diag(pool_w).astype(BF16),
        pool_scale=row(pool_scale),
        wl=jnp.repeat(jnp.asarray(POOL_WINDOWS, F32), POOL_GC)[None, :],
        sconv_w=sconv_w, cconv_w=cconv_w, cconv_b=row(cconv_b),
        cln_g=row(cln_g), cln_b=row(cln_b),
        ffn_conv_w=ffn_conv_w,
    )
    names = ("w_in", "w_out", "w_up", "w_down")
    stacked = (w_in, w_out, w_up, w_down)
    w = dict(zip(names, (_interleave_glu_columns(w_in[0]).astype(BF16), w_out[0].astype(BF16),
                         w_up[0].astype(BF16), w_down[0].astype(BF16))))
    spool, sz, sglu, sup = (jnp.swapaxes(s, 1, 2)
                            for s in (state_pool, state_sconv, state_cconv, state_ffn))

    xp = _tile_permute(x_prompt)
    xs = jnp.swapaxes(x_sample, 0, 1).reshape(DEC_ROWS, D_MODEL)
    tails = [[] for _ in range(4)]
    news = [[] for _ in range(4)]
    for l in range(DEPTH):
        xp, tpool, tz, tglu, *w_next = _mix_prompt_call(
            l, xp, mod, p, w, stacked if l + 1 < DEPTH else None)
        xp, tup = _ffn_prompt_call(l, xp, mod, p, w)
        for acc, a in zip(tails, (tpool, tz, tglu, tup)):
            acc.append(a)
        xs, npool, nz, nglu = _mix_dec_call(l, xs, mod, spool, sz, sglu, p, w)
        xs, nup_gate, nup_val = _ffn_dec_call(l, xs, mod, sup, p, w)
        w = dict(zip(names, w_next))
        nup = jnp.concatenate([nup_gate, nup_val], axis=-1)
        for acc, a in zip(news, (npool, nz, nglu, nup)):
            acc.append(a)

    prompt_state = [jnp.stack(a) for a in tails]
    dec_state = [jnp.swapaxes(jnp.stack(a), 1, 2) for a in news]
    ys = jnp.swapaxes(xs.reshape(DEC_SEQ, DEC_BATCH, D_MODEL), 0, 1)
    return (_tile_permute(xp, inverse=True), ys) + tuple(prompt_state) + tuple(dec_state)
```

```python
import functools

import jax
import jax.numpy as jnp
from jax import lax
from jax.experimental import pallas as pl
from jax.experimental.pallas import tpu as pltpu

D_MODEL = 1024
BATCH = 8
SEQ = 2048
DEPTH = 4
DEC_BATCH = 128
DEC_SEQ = 4
PAST_LEN = 16384

D_POOL = 256
POOL_WINDOWS = (2, 4, 8, 16)
POOL_GC = 64
POOL_PAST = 15
D_SCONV = 384
D_CCONV = 384
SCONV_K = 3
CCONV_K = 31
FFN_K = 3
D_FF = 2816
D_IN = D_POOL + 3 * D_SCONV + 2 * D_CCONV
EPS = 1e-6

LANES = 128
SUBLANES = 8

O_HB = D_POOL
O_BG = O_HB + D_SCONV
O_CG = O_BG + D_SCONV
O_AC = O_CG + D_SCONV
O_BC = O_AC + D_CCONV
O_GLU = O_AC
N_GLU_TILES = D_CCONV // LANES

FF_CHUNK = 256
N_FF_CHUNKS = D_FF // FF_CHUNK
TM = 1024
N_SEQ_TILES = SEQ // TM
SUB = 512
N_SUB = TM // SUB
SEG = SUB // SUBLANES
DEC_ROWS = DEC_BATCH * DEC_SEQ
N_ADA = BATCH + DEC_BATCH
ADA_TN = 1024
VMEM_LIMIT_BYTES = 56 * 1024 * 1024

F32 = jnp.float32
BF16 = jnp.bfloat16

assert 2 * POOL_GC == LANES and len(POOL_WINDOWS) == 4
assert SEG >= CCONV_K - 1


def _rms_scaled(x, scale):
    return x * lax.rsqrt(jnp.mean(x * x, axis=-1, keepdims=True) + EPS) * scale


def _layer_norm(x, g, b):
    mu = jnp.mean(x, axis=-1, keepdims=True)
    xc = x - mu
    var = jnp.mean(xc * xc, axis=-1, keepdims=True)
    return xc * lax.rsqrt(var + EPS) * g + b


def _silu(x):
    return x * jax.nn.sigmoid(x)


def _dot(a, b):
    return jnp.dot(a, b, preferred_element_type=F32)


def _layer_spec(shape, l, **kw):
    return pl.BlockSpec((None,) + shape, lambda *_: (l,) + (0,) * len(shape), **kw)


def _whole_spec(shape, **kw):
    return pl.BlockSpec(shape, lambda *_: (0,) * len(shape), **kw)


def _ada_kernel(c_ref, w_ref, b_ref, o_ref):
    s = _silu(c_ref[...]).astype(BF16)
    o_ref[...] = _dot(s, w_ref[...].astype(BF16)) + b_ref[...]


def _ada_call(c_all, w_ada, b_ada):
    return pl.pallas_call(
        _ada_kernel,
        grid=(DEPTH, 6 * D_MODEL // ADA_TN),
        in_specs=[
            pl.BlockSpec((N_ADA, D_MODEL), lambda l, n: (0, 0)),
            pl.BlockSpec((None, D_MODEL, ADA_TN), lambda l, n: (l, 0, n)),
            pl.BlockSpec((None, 1, ADA_TN), lambda l, n: (l, 0, n)),
        ],
        out_specs=pl.BlockSpec((None, N_ADA, ADA_TN), lambda l, n: (l, 0, n)),
        out_shape=jax.ShapeDtypeStruct((DEPTH, N_ADA, 6 * D_MODEL), F32),
        compiler_params=pltpu.CompilerParams(
            dimension_semantics=("arbitrary", "arbitrary"),
            vmem_limit_bytes=VMEM_LIMIT_BYTES),
        name="ada",
    )(c_all, w_ada, b_ada.reshape(DEPTH, 1, 6 * D_MODEL))


def _mixer_heads(proj):
    u = proj[:, 0:O_HB]
    z_b = proj[:, O_CG:O_AC] * proj[:, O_HB:O_BG]
    bg = proj[:, O_BG:O_CG]
    glu = []
    for q in range(N_GLU_TILES):
        a = proj[:, O_GLU + 2 * q * LANES:O_GLU + (2 * q + 1) * LANES]
        b = proj[:, O_GLU + (2 * q + 1) * LANES:O_GLU + (2 * q + 2) * LANES]
        glu.append(a * jax.nn.sigmoid(b))
    return u, z_b, bg, jnp.concatenate(glu, axis=-1)


def _pool_windows(tap):
    halves = []
    for h in range(2):
        w_small, w_big = POOL_WINDOWS[2 * h], POOL_WINDOWS[2 * h + 1]
        cols = slice(h * LANES, (h + 1) * LANES)
        s = tap(0, cols)
        for i in range(1, w_small):
            s = s + tap(i, cols)
        small = s
        for i in range(w_small, w_big):
            s = s + tap(i, cols)
        lane = lax.broadcasted_iota(jnp.int32, s.shape, 1)
        halves.append(jnp.where(lane < POOL_GC, small, s))
    return jnp.concatenate(halves, axis=-1)


def _mixer_out(x, gt1, gpost_ref, win, cnt, u, conv_b, bg, conv_c,
               wpool_ref, pscale_ref, cb_ref, lg_ref, lb_ref, wout_ref):
    pooled = (win / cnt - u).astype(BF16)
    y_a = (_dot(pooled, wpool_ref[...]) * pscale_ref[...]).astype(BF16)
    y_b = (bg * conv_b).astype(BF16)
    y_c = _silu(_layer_norm(conv_c + cb_ref[...], lg_ref[...], lb_ref[...])).astype(BF16)
    mix = _dot(jnp.concatenate([y_a, y_b, y_c], axis=-1), wout_ref[...])
    return x + _rms_scaled(mix, gt1 * gpost_ref[...])


def _halo_groups(prev_tail, cur, n_back):
    tail = cur[SUB - SUBLANES * n_back:]
    sub = lax.broadcasted_iota(jnp.int32, tail.shape, 0) & (SUBLANES - 1)
    mixed = jnp.where(sub == SUBLANES - 1, prev_tail, tail)
    groups = [pltpu.roll(mixed[SUBLANES * i:SUBLANES * (i + 1)], 1, axis=0) for i in range(n_back)]
    return jnp.concatenate(groups, axis=0)


def _last_positions(out_ref, tail_ref):
    for k in range(out_ref.shape[1]):
        row = SUBLANES * k + SUBLANES - 1
        out_ref[0, k:k + 1, :] = tail_ref[row:row + 1, :]


N_LANE_TILES = D_MODEL // LANES
REORDER_BUF = (N_LANE_TILES, SUB, LANES)


def _load_natural(ref, start, buf):
    for g in range(SUB // SUBLANES):
        s, k = divmod(g, SEG // SUBLANES)
        rows = ref[0, start + SUBLANES * g:start + SUBLANES * (g + 1), :]
        for q in range(N_LANE_TILES):
            buf[q, pl.ds(SUBLANES * SUBLANES * k + s, SUBLANES, stride=SUBLANES), :] = (
                rows[:, q * LANES:(q + 1) * LANES])
    return jnp.concatenate([buf[q] for q in range(N_LANE_TILES)], axis=-1)


def _store_natural(ref, start, val, buf):
    for q in range(N_LANE_TILES):
        buf[q] = val[:, q * LANES:(q + 1) * LANES]
    for g in range(SUB // SUBLANES):
        s, k = divmod(g, SEG // SUBLANES)
        rows = [buf[q, pl.ds(SUBLANES * SUBLANES * k + s, SUBLANES, stride=SUBLANES), :]
                for q in range(N_LANE_TILES)]
        ref[0, start + SUBLANES * g:start + SUBLANES * (g + 1), :] = jnp.concatenate(rows, axis=-1)


def _back(ext, n_back, j):
    lo = SUBLANES * (n_back - j)
    return ext[lo:lo + SUB]


def _mix_prompt_kernel(x_ref, mod_ref, gpre_ref, gpost_ref, win_ref, wpool_ref, pscale_ref,
                       wl_ref, sw_ref, cw_ref, cb_ref, lg_ref, lb_ref, wout_ref, *rest,
                       cast_glu, natural_in):
    b = pl.program_id(0)
    t = pl.program_id(1)
    n_cast = len(cast_glu)
    cast_f32, rest = rest[:n_cast], rest[n_cast:]
    xo_ref, tpool_ref, tz_ref, tglu_ref = rest[:4]
    cast_bf16, rest = rest[4:4 + n_cast], rest[4 + n_cast:]
    for src, dst, glu in zip(cast_f32, cast_bf16, cast_glu):
        block = src[...]
        dst[...] = (_interleave_glu_columns(block) if glu else block).astype(BF16)
    tail_u, tail_z, tail_glu, reorder_buf = rest

    @pl.when(t == 0)
    def _():
        tail_u[...] = jnp.zeros(tail_u.shape, F32)
        tail_z[...] = jnp.zeros(tail_z.shape, F32)
        tail_glu[...] = jnp.zeros(tail_glu.shape, F32)

    sh1 = mod_ref[pl.ds(b, 1), 0:D_MODEL]
    sc1 = mod_ref[pl.ds(b, 1), D_MODEL:2 * D_MODEL]
    gt1 = mod_ref[pl.ds(b, 1), 2 * D_MODEL:3 * D_MODEL]
    pre_scale = gpre_ref[...] * (1.0 + sc1)

    xs, heads = [], []
    for part in range(N_SUB):
        if natural_in:
            x = _load_natural(x_ref, part * SUB, reorder_buf)
        else:
            x = x_ref[0, part * SUB:(part + 1) * SUB, :]
        h = _rms_scaled(x, pre_scale) + sh1
        xs.append(x)
        heads.append(_mixer_heads(_dot(h.astype(BF16), win_ref[...])))

    for part in range(N_SUB):
        rows = slice(part * SUB, (part + 1) * SUB)
        x = xs[part]
        u, z_b, bg, glu = heads[part]

        eu = jnp.concatenate([_halo_groups(tail_u[...], u, POOL_PAST), u], axis=0)
        ez = jnp.concatenate([_halo_groups(tail_z[...], z_b, SCONV_K - 1), z_b], axis=0)
        eg = jnp.concatenate([_halo_groups(tail_glu[...], glu, CCONV_K - 1), glu], axis=0)
        tail_u[...] = u[SUB - SUBLANES * POOL_PAST:]
        tail_z[...] = z_b[SUB - SUBLANES * (SCONV_K - 1):]
        tail_glu[...] = glu[SUB - SUBLANES * (CCONV_K - 1):]

        win = _pool_windows(lambda i, cols: _back(eu, POOL_PAST, i)[:, cols])
        row = lax.broadcasted_iota(jnp.int32, (SUB, D_POOL), 0)
        pos1 = (t * N_SUB + part) * SUB + 1 + (row & (SUBLANES - 1)) * SEG + (row >> 3)
        cnt = jnp.minimum(pos1.astype(F32), wl_ref[...])

        conv_b = sw_ref[SCONV_K - 1:SCONV_K, :] * z_b
        for k in range(SCONV_K - 1):
            conv_b = conv_b + sw_ref[k:k + 1, :] * _back(ez, SCONV_K - 1, SCONV_K - 1 - k)
        conv_c = cw_ref[CCONV_K - 1:CCONV_K, :] * glu
        for k in range(CCONV_K - 1):
            conv_c = conv_c + cw_ref[k:k + 1, :] * _back(eg, CCONV_K - 1, CCONV_K - 1 - k)

        xo_ref[0, rows, :] = _mixer_out(x, gt1, gpost_ref, win, cnt, u, conv_b, bg, conv_c,
                                        wpool_ref, pscale_ref, cb_ref, lg_ref, lb_ref, wout_ref)

    @pl.when(t == N_SEQ_TILES - 1)
    def _():
        _last_positions(tpool_ref, tail_u)
        _last_positions(tz_ref, tail_z)
        _last_positions(tglu_ref, tail_glu)


def _mix_prompt_call(l, x, mod, p, w, casts):
    tile = pl.BlockSpec((1, TM, D_MODEL), lambda b, t: (b, t, 0))
    tail = lambda n_back, ch: pl.BlockSpec((1, n_back, ch), lambda b, t: (b, 0, 0))
    tail_shape = lambda n_back, ch: (BATCH, n_back, ch)
    once = dict(pipeline_mode=pl.Buffered(1))
    ls = functools.partial(_layer_spec, l=l)
    n_steps = BATCH * N_SEQ_TILES
    cast_in, cast_out, cast_shape = [], [], []
    step_block = lambda b, t, layer: (layer, b * N_SEQ_TILES + t, 0)
    for a, layer, _ in casts:
        rows, cols = a.shape[1] // n_steps, a.shape[2]
        cast_in.append(pl.BlockSpec((None, rows, cols),
                                    functools.partial(step_block, layer=layer)))
        cast_out.append(pl.BlockSpec((rows, cols), lambda b, t: (b * N_SEQ_TILES + t, 0)))
        cast_shape.append(jax.ShapeDtypeStruct(a.shape[1:], BF16))
    return pl.pallas_call(
        functools.partial(_mix_prompt_kernel, cast_glu=tuple(glu for _, _, glu in casts),
                          natural_in=l == 0),
        grid=(BATCH, N_SEQ_TILES),
        in_specs=[
            tile,
            pl.BlockSpec((None, BATCH, 6 * D_MODEL), lambda b, t: (l, DEC_BATCH // BATCH, 0)),
            ls((1, D_MODEL)), ls((1, D_MODEL)),
            _whole_spec((D_MODEL, D_IN), **once),
            ls((D_POOL, D_POOL)), ls((1, D_POOL)),
            pl.BlockSpec((1, D_POOL), lambda b, t: (0, 0)),
            ls((SCONV_K, D_SCONV)),
            ls((CCONV_K, D_CCONV)), ls((1, D_CCONV)), ls((1, D_CCONV)), ls((1, D_CCONV)),
            _whole_spec((D_MODEL, D_MODEL), **once),
        ] + cast_in,
        out_specs=[tile, tail(POOL_PAST, D_POOL), tail(SCONV_K - 1, D_SCONV),
                   tail(CCONV_K - 1, D_CCONV)] + cast_out,
        out_shape=[
            jax.ShapeDtypeStruct((BATCH, SEQ, D_MODEL), F32),
            jax.ShapeDtypeStruct(tail_shape(POOL_PAST, D_POOL), F32),
            jax.ShapeDtypeStruct(tail_shape(SCONV_K - 1, D_SCONV), F32),
            jax.ShapeDtypeStruct(tail_shape(CCONV_K - 1, D_CCONV), F32),
        ] + cast_shape,
        scratch_shapes=[
            pltpu.VMEM((SUBLANES * POOL_PAST, D_POOL), F32),
            pltpu.VMEM((SUBLANES * (SCONV_K - 1), D_SCONV), F32),
            pltpu.VMEM((SUBLANES * (CCONV_K - 1), D_CCONV), F32),
            pltpu.VMEM(REORDER_BUF, F32),
        ],
        compiler_params=pltpu.CompilerParams(
            dimension_semantics=("arbitrary", "arbitrary"),
            vmem_limit_bytes=VMEM_LIMIT_BYTES),
        name="mix_prompt",
    )(x, mod, p["g_mix_pre"], p["g_mix_post"], w["w_in"], p["wpool"], p["pool_scale"],
      p["wl"], p["sconv_w"], p["cconv_w"], p["cconv_b"], p["cln_g"], p["cln_b"], w["w_out"],
      *(a for a, _, _ in casts))


def _ffn_prompt_kernel(x_ref, mod_ref, gpre_ref, gpost_ref, wu_ref, fw_ref, wd_ref,
                       xo_ref, tup_ref, tail_up, act_buf, reorder_buf, *, natural_out):
    b = pl.program_id(0)
    t = pl.program_id(1)
    n_back = FFN_K - 1

    @pl.when(t == 0)
    def _():
        tail_up[...] = jnp.zeros(tail_up.shape, F32)

    x = x_ref[0]
    sh2 = mod_ref[pl.ds(b, 1), 3 * D_MODEL:4 * D_MODEL]
    sc2 = mod_ref[pl.ds(b, 1), 4 * D_MODEL:5 * D_MODEL]
    gt2 = mod_ref[pl.ds(b, 1), 5 * D_MODEL:6 * D_MODEL]
    hb = (_rms_scaled(x, gpre_ref[...] * (1.0 + sc2)) + sh2).astype(BF16)

    def conv(lo):
        cols = slice(lo, lo + FF_CHUNK)
        up = _dot(hb, wu_ref[:, cols])
        prev = tail_up[:, cols]
        outs = []
        for part in range(N_SUB):
            cur = up[part * SUB:(part + 1) * SUB]
            ext = jnp.concatenate([_halo_groups(prev, cur, n_back), cur], axis=0)
            prev = cur[SUB - SUBLANES * n_back:]
            out = fw_ref[FFN_K - 1:FFN_K, cols] * cur
            for k in range(n_back):
                out = out + fw_ref[k:k + 1, cols] * _back(ext, n_back, n_back - k)
            outs.append(out)
        tail_up[:, cols] = prev
        return jnp.concatenate(outs, axis=0)

    for j in range(N_FF_CHUNKS):
        lo = j * FF_CHUNK
        act_buf[:, lo:lo + FF_CHUNK] = (_silu(conv(lo)) * conv(D_FF + lo)).astype(BF16)
    ff = _dot(act_buf[...], wd_ref[...])

    out = x + _rms_scaled(ff, gt2 * gpost_ref[...])
    if natural_out:
        for part in range(N_SUB):
            _store_natural(xo_ref, part * SUB, out[part * SUB:(part + 1) * SUB], reorder_buf)
    else:
        xo_ref[0] = out

    @pl.when(t == N_SEQ_TILES - 1)
    def _():
        _last_positions(tup_ref, tail_up)


def _ffn_prompt_call(l, x, mod, p, w):
    tile = pl.BlockSpec((1, TM, D_MODEL), lambda b, t: (b, t, 0))
    ls = functools.partial(_layer_spec, l=l)
    once = dict(pipeline_mode=pl.Buffered(1))
    n_back = FFN_K - 1
    return pl.pallas_call(
        functools.partial(_ffn_prompt_kernel, natural_out=l == DEPTH - 1),
        grid=(BATCH, N_SEQ_TILES),
        in_specs=[
            tile,
            pl.BlockSpec((None, BATCH, 6 * D_MODEL), lambda b, t: (l, DEC_BATCH // BATCH, 0)),
            ls((1, D_MODEL)), ls((1, D_MODEL)),
            _whole_spec((D_MODEL, 2 * D_FF), **once),
            ls((FFN_K, 2 * D_FF)),
            _whole_spec((D_FF, D_MODEL), **once),
        ],
        out_specs=[tile, pl.BlockSpec((1, n_back, 2 * D_FF), lambda b, t: (b, 0, 0))],
        out_shape=[
            jax.ShapeDtypeStruct((BATCH, SEQ, D_MODEL), F32),
            jax.ShapeDtypeStruct((BATCH, n_back, 2 * D_FF), F32),
        ],
        scratch_shapes=[pltpu.VMEM((SUBLANES * n_back, 2 * D_FF), F32),
                        pltpu.VMEM((TM, D_FF), BF16),
                        pltpu.VMEM(REORDER_BUF, F32)],
        compiler_params=pltpu.CompilerParams(
            dimension_semantics=("arbitrary", "arbitrary"),
            vmem_limit_bytes=VMEM_LIMIT_BYTES),
        name="ffn_prompt",
    )(x, mod, p["g_ffn_pre"], p["g_ffn_post"], w["w_up"], p["ffn_conv_w"], w["w_down"])


def _slab(a, t):
    return a[t * DEC_BATCH:(t + 1) * DEC_BATCH]


def _mix_dec_kernel(x_ref, mod_ref, gpre_ref, gpost_ref, win_ref, wpool_ref, pscale_ref,
                    wl_ref, sw_ref, cw_ref, cb_ref, lg_ref, lb_ref, wout_ref,
                    spool_ref, sz_ref, sglu_ref,
                    xo_ref, npool_ref, nz_ref, nglu_ref):
    x = x_ref[...]
    rep = lambda a: jnp.concatenate([a] * DEC_SEQ, axis=0)
    sh1 = rep(mod_ref[:, 0:D_MODEL])
    sc1 = rep(mod_ref[:, D_MODEL:2 * D_MODEL])
    gt1 = rep(mod_ref[:, 2 * D_MODEL:3 * D_MODEL])
    h = _rms_scaled(x, gpre_ref[...] * (1.0 + sc1)) + sh1
    u, z_b, bg, glu = _mixer_heads(_dot(h.astype(BF16), win_ref[...]))

    def ext(state_ref, cur, n_past, j):
        return state_ref[j] if j < n_past else _slab(cur, j - n_past)

    for new_ref, state_ref, cur in ((npool_ref, spool_ref, u), (nz_ref, sz_ref, z_b),
                                    (nglu_ref, sglu_ref, glu)):
        n_past = state_ref.shape[0]
        for k in range(n_past):
            new_ref[k] = ext(state_ref, cur, n_past, k + DEC_SEQ)

    wins, convs_b, convs_c, cnts = [], [], [], []
    for t in range(DEC_SEQ):
        wins.append(_pool_windows(
            lambda i, cols: ext(spool_ref, u, POOL_PAST, POOL_PAST + t - i)[:, cols]))
        cnts.append(jnp.broadcast_to(jnp.minimum(F32(PAST_LEN + t + 1), wl_ref[...]),
                                     (DEC_BATCH, D_POOL)))
        acc = None
        for k in range(SCONV_K):
            term = sw_ref[k:k + 1, :] * ext(sz_ref, z_b, SCONV_K - 1, t + k)
            acc = term if acc is None else acc + term
        convs_b.append(acc)
        acc = None
        for k in range(CCONV_K):
            term = cw_ref[k:k + 1, :] * ext(sglu_ref, glu, CCONV_K - 1, t + k)
            acc = term if acc is None else acc + term
        convs_c.append(acc)
    cat = lambda xs: jnp.concatenate(xs, axis=0)
    xo_ref[...] = _mixer_out(x, gt1, gpost_ref, cat(wins), cat(cnts), u, cat(convs_b), bg,
                             cat(convs_c), wpool_ref, pscale_ref, cb_ref, lg_ref, lb_ref,
                             wout_ref)


def _full_spec(shape):
    return pl.BlockSpec(shape, lambda i: (0,) * len(shape))


def _mix_dec_call(l, x, mod, spool, sz, sglu, p, w):
    ls = functools.partial(_layer_spec, l=l)
    states = (spool, sz, sglu)
    return pl.pallas_call(
        _mix_dec_kernel,
        grid=(1,),
        in_specs=[
            _full_spec((DEC_ROWS, D_MODEL)), ls((DEC_BATCH, 6 * D_MODEL)),
            ls((1, D_MODEL)), ls((1, D_MODEL)),
            _whole_spec((D_MODEL, D_IN)),
            ls((D_POOL, D_POOL)), ls((1, D_POOL)),
            _full_spec((1, D_POOL)),
            ls((SCONV_K, D_SCONV)),
            ls((CCONV_K, D_CCONV)), ls((1, D_CCONV)), ls((1, D_CCONV)), ls((1, D_CCONV)),
            _whole_spec((D_MODEL, D_MODEL)),
        ] + [ls(s.shape[1:]) for s in states],
        out_specs=[_full_spec((DEC_ROWS, D_MODEL))] + [_full_spec(s.shape[1:]) for s in states],
        out_shape=[jax.ShapeDtypeStruct((DEC_ROWS, D_MODEL), F32)]
        + [jax.ShapeDtypeStruct(s.shape[1:], F32) for s in states],
        compiler_params=pltpu.CompilerParams(
            dimension_semantics=("arbitrary",), vmem_limit_bytes=VMEM_LIMIT_BYTES),
        name="mix_dec",
    )(x, mod, p["g_mix_pre"], p["g_mix_post"], w["w_in"], p["wpool"], p["pool_scale"],
      p["wl"], p["sconv_w"], p["cconv_w"], p["cconv_b"], p["cln_g"], p["cln_b"],
      w["w_out"], *states)


def _ffn_dec_kernel(x_ref, mod_ref, gpre_ref, gpost_ref, wu_ref, fw_ref, wd_ref, sup_ref,
                    xo_ref, up_ref, act_buf):
    x = x_ref[...]
    rep = lambda a: jnp.concatenate([a] * DEC_SEQ, axis=0)
    sh2 = rep(mod_ref[:, 3 * D_MODEL:4 * D_MODEL])
    sc2 = rep(mod_ref[:, 4 * D_MODEL:5 * D_MODEL])
    gt2 = rep(mod_ref[:, 5 * D_MODEL:6 * D_MODEL])
    hb = (_rms_scaled(x, gpre_ref[...] * (1.0 + sc2)) + sh2).astype(BF16)
    n_past = FFN_K - 1

    def conv(lo):
        cols = slice(lo, lo + FF_CHUNK)
        cur = _dot(hb, wu_ref[:, cols])
        for k in range(n_past):
            up_ref[k, :, cols] = _slab(cur, DEC_SEQ - n_past + k)
        outs = []
        for t in range(DEC_SEQ):
            acc = None
            for k in range(FFN_K):
                j = t + k
                row = sup_ref[j, :, cols] if j < n_past else _slab(cur, j - n_past)
                term = fw_ref[k:k + 1, cols] * row
                acc = term if acc is None else acc + term
            outs.append(acc)
        return jnp.concatenate(outs, axis=0)

    for j in range(N_FF_CHUNKS):
        lo = j * FF_CHUNK
        act_buf[:, lo:lo + FF_CHUNK] = (_silu(conv(lo)) * conv(D_FF + lo)).astype(BF16)
    ff = _dot(act_buf[...], wd_ref[...])
    xo_ref[...] = x + _rms_scaled(ff, gt2 * gpost_ref[...])


def _ffn_dec_call(l, x, mod, sup, p, w):
    ls = functools.partial(_layer_spec, l=l)
    return pl.pallas_call(
        _ffn_dec_kernel,
        grid=(1,),
        in_specs=[
            _full_spec((DEC_ROWS, D_MODEL)), ls((DEC_BATCH, 6 * D_MODEL)),
            ls((1, D_MODEL)), ls((1, D_MODEL)),
            _whole_spec((D_MODEL, 2 * D_FF)),
            ls((FFN_K, 2 * D_FF)),
            _whole_spec((D_FF, D_MODEL)),
            ls(sup.shape[1:]),
        ],
        out_specs=[_full_spec((DEC_ROWS, D_MODEL)), _full_spec(sup.shape[1:])],
        out_shape=[
            jax.ShapeDtypeStruct((DEC_ROWS, D_MODEL), F32),
            jax.ShapeDtypeStruct(sup.shape[1:], F32),
        ],
        scratch_shapes=[pltpu.VMEM((DEC_ROWS, D_FF), BF16)],
        compiler_params=pltpu.CompilerParams(
            dimension_semantics=("arbitrary",), vmem_limit_bytes=VMEM_LIMIT_BYTES),
        name="ffn_dec",
    )(x, mod, p["g_ffn_pre"], p["g_ffn_post"], w["w_up"], p["ffn_conv_w"], w["w_down"], sup)


def _block_diag(pool_w):
    g = len(POOL_WINDOWS)
    eye = jnp.eye(g, dtype=bool)[None, :, None, :, None]
    full = jnp.where(eye, pool_w[:, :, :, None, :], 0.0)
    return full.reshape(DEPTH, D_POOL, D_POOL)


def _interleave_glu_columns(w_in):
    parts = [w_in[..., :O_AC]]
    for q in range(N_GLU_TILES):
        parts.append(w_in[..., O_AC + q * LANES:O_AC + (q + 1) * LANES])
        parts.append(w_in[..., O_BC + q * LANES:O_BC + (q + 1) * LANES])
    return jnp.concatenate(parts, axis=-1)


def kernel(x_prompt, x_sample, c_prompt, c_sample, state_pool, state_sconv, state_cconv, state_ffn,
           w_ada, b_ada, g_mix_pre, g_mix_post, g_ffn_pre, g_ffn_post, w_in, pool_w, pool_scale,
           sconv_w, cconv_w, cconv_b, cln_g, cln_b, w_out, w_up, ffn_conv_w, w_down):
    mod = _ada_call(jnp.concatenate([c_sample, c_prompt], axis=0), w_ada, b_ada)
    row = lambda a: a[:, None, :]
    p = dict(
        g_mix_pre=row(g_mix_pre), g_mix_post=row(g_mix_post),
        g_ffn_pre=row(g_ffn_pre), g_ffn_post=row(g_ffn_post),
        wpool=_block_diag(pool_w).astype(BF16),
        pool_scale=row(pool_scale),
        wl=jnp.repeat(jnp.asarray(POOL_WINDOWS, F32), POOL_GC)[None, :],
        sconv_w=sconv_w, cconv_w=cconv_w, cconv_b=row(cconv_b),
        cln_g=row(cln_g), cln_b=row(cln_b),
        ffn_conv_w=ffn_conv_w,
    )
    stacked = dict(w_in=w_in, w_out=w_out, w_up=w_up, w_down=w_down)
    w = dict(w_in=_interleave_glu_columns(w_in[0]).astype(BF16), w_out=w_out[0].astype(BF16))
    spool, sz, sglu, sup = (jnp.swapaxes(s, 1, 2)
                            for s in (state_pool, state_sconv, state_cconv, state_ffn))

    xp = x_prompt
    xs = jnp.swapaxes(x_sample, 0, 1).reshape(DEC_ROWS, D_MODEL)
    tails = [[] for _ in range(4)]
    news = [[] for _ in range(4)]
    for l in range(DEPTH):
        jobs = [(n, l) for n in ("w_up", "w_down") if l == 0]
        jobs += [(n, l + 1) for n in stacked if l + 1 < DEPTH]
        xp, tpool, tz, tglu, *rounded = _mix_prompt_call(
            l, xp, mod, p, w, [(stacked[n], layer, n == "w_in") for n, layer in jobs])
        w.update({n: a for (n, layer), a in zip(jobs, rounded) if layer == l})
        xp, tup = _ffn_prompt_call(l, xp, mod, p, w)
        for acc, a in zip(tails, (tpool, tz, tglu, tup)):
            acc.append(a)
        xs, npool, nz, nglu = _mix_dec_call(l, xs, mod, spool, sz, sglu, p, w)
        xs, nup = _ffn_dec_call(l, xs, mod, sup, p, w)
        w = {n: a for (n, layer), a in zip(jobs, rounded) if layer == l + 1}
        for acc, a in zip(news, (npool, nz, nglu, nup)):
            acc.append(a)

    prompt_state = [jnp.stack(a) for a in tails]
    dec_state = [jnp.swapaxes(jnp.stack(a), 1, 2) for a in news]
    ys = jnp.swapaxes(xs.reshape(DEC_SEQ, DEC_BATCH, D_MODEL), 0, 1)
    return (xp, ys) + tuple(prompt_state) + tuple(dec_state)
```
